```python
import math
import jax, jax.numpy as jnp
from jax import lax
import numpy as np

D_MODEL = 1024
BATCH = 8
SEQ = 2048
DEPTH = 2

RG_WIDTH = 512
RG_BLOCKS = 4
RG_BLOCK = RG_WIDTH // RG_BLOCKS
RG_CONV = 4
RG_C = 8.0
DA_HEADS = 4
DA_HEAD_DIM = 64
DA_WIDTH = DA_HEADS * 2 * DA_HEAD_DIM
ROPE_THETA = 500000.0
ROPE_DIM = DA_HEAD_DIM // 4
Q_BLOCK = 128
NEG_INF = -1e30
ML_HEADS = 4
ML_HEAD_DIM = 128
ML_WIDTH = ML_HEADS * ML_HEAD_DIM
ML_CONV = 4
ML_CHUNK = 64
D_MIX = RG_WIDTH + DA_WIDTH + ML_WIDTH
D_FF = 2816
FFN_CONV = 3
EPS = 1e-6
IN_WIDTHS = (RG_WIDTH, RG_WIDTH, DA_WIDTH, DA_WIDTH, DA_WIDTH,
             ML_WIDTH, ML_WIDTH, ML_WIDTH, ML_WIDTH, ML_HEADS, ML_HEADS)
D_IN = sum(IN_WIDTHS)

kernel_name = 'hybrid_parallel_heads_rglru_diffattn_mlstm'


def rmsnorm(x, g):
    xf = x.astype(jnp.float32)
    y = xf * lax.rsqrt(jnp.mean(xf * xf, axis=-1, keepdims=True) + EPS)
    return (y * g.astype(jnp.float32)).astype(x.dtype)


def causal_dwconv(x, w, b):
    K = w.shape[0]
    S = x.shape[1]
    xp = jnp.pad(x, ((0, 0), (K - 1, 0), (0, 0)))
    return sum(xp[:, j:j + S] * w[j] for j in range(K)) + b


def rope_tables(positions):
    inv_freq = ROPE_THETA ** (-jnp.arange(0, ROPE_DIM, 2, dtype=jnp.float32) / ROPE_DIM)
    ang = positions.astype(jnp.float32)[..., None] * inv_freq
    return jnp.cos(ang), jnp.sin(ang)


def apply_partial_rope(x, cos, sin):
    half = ROPE_DIM // 2
    xr = x[..., :ROPE_DIM].astype(jnp.float32)
    x1, x2 = xr[..., :half], xr[..., half:]
    rot = jnp.concatenate([x1 * cos - x2 * sin, x2 * cos + x1 * sin], axis=-1)
    return jnp.concatenate([rot.astype(x.dtype), x[..., ROPE_DIM:]], axis=-1)


def rglru_group(xb, gb, conv_w, conv_b, wa, ba, wx, bx, lam, norm_g):
    B, S, _ = xb.shape
    u = causal_dwconv(xb, conv_w, conv_b)
    ub = u.reshape(B, S, RG_BLOCKS, RG_BLOCK)
    r = jax.nn.sigmoid(jnp.einsum('bsni,nij->bsnj', ub, wa).reshape(B, S, RG_WIDTH) + ba)
    i = jax.nn.sigmoid(jnp.einsum('bsni,nij->bsnj', ub, wx).reshape(B, S, RG_WIDTH) + bx)
    log_a = RG_C * r.astype(jnp.float32) * jax.nn.log_sigmoid(lam.astype(jnp.float32))
    a = jnp.exp(log_a)
    bt = jnp.sqrt(-jnp.expm1(2.0 * log_a)) * (i * u).astype(jnp.float32)

    def combine(left, right):
        a1, b1 = left
        a2, b2 = right
        return a1 * a2, a2 * b1 + b2

    _, h = lax.associative_scan(combine, (a, bt), axis=1)
    y = jax.nn.gelu(gb.astype(jnp.float32)) * h
    return rmsnorm(y.astype(xb.dtype), norm_g)


def diff_attention_group(q, k, v, positions, lam_p, norm_g, lambda_init):
    B, S, _ = q.shape
    H, dh = DA_HEADS, DA_HEAD_DIM
    q = q.reshape(B, S, H, 2, dh)
    k = k.reshape(B, S, H, 2, dh)
    v = v.reshape(B, S, H, 2 * dh)
    cos, sin = rope_tables(positions)
    cos, sin = cos[:, :, None, None, :], sin[:, :, None, None, :]
    q = apply_partial_rope(q, cos, sin)
    k = apply_partial_rope(k, cos, sin)
    lp = lam_p.astype(jnp.float32)
    lam = jnp.exp(jnp.sum(lp[0] * lp[1])) - jnp.exp(jnp.sum(lp[2] * lp[3])) + lambda_init
    nb = S // Q_BLOCK
    qb = q.reshape(B, nb, Q_BLOCK, H, 2, dh).transpose(1, 0, 2, 3, 4, 5)
    key_pos = jnp.arange(S)
    scale = dh ** -0.5

    def block(args):
        qi, bi = args
        s = jnp.einsum('bqhcd,bkhcd->bhcqk', qi, k).astype(jnp.float32) * scale
        qpos = bi * Q_BLOCK + jnp.arange(Q_BLOCK)
        mask = qpos[:, None] >= key_pos[None, :]
        p = jax.nn.softmax(jnp.where(mask, s, NEG_INF), axis=-1)
        p = p[:, :, 0] - lam * p[:, :, 1]
        return jnp.einsum('bhqk,bkhe->bqhe', p.astype(v.dtype), v)

    o = lax.map(block, (qb, jnp.arange(nb)))
    o = o.transpose(1, 0, 2, 3, 4).reshape(B, S, H, 2 * dh)
    o = rmsnorm(o, norm_g) * (1.0 - lambda_init)
    return o.reshape(B, S, DA_WIDTH)


def mlstm_chunk_step(carry, xs):
    C, n, m = carry
    qc, kc, vc, li, lf = xs
    L = qc.shape[2]
    b = jnp.cumsum(lf, axis=-1)
    causal = jnp.tril(jnp.ones((L, L), dtype=bool))
    D = jnp.where(causal, b[..., :, None] - b[..., None, :] + li[..., None, :], -jnp.inf)
    m_inter = b + m[..., None]
    m_t = jnp.maximum(m_inter, jnp.max(D, axis=-1))
    w_intra = jnp.einsum('bhtd,bhsd->bhts', qc, kc) * jnp.exp(D - m_t[..., None])
    inter = jnp.exp(m_inter - m_t)
    num = inter[..., None] * jnp.einsum('bhtd,bhde->bhte', qc, C) + jnp.einsum('bhts,bhse->bhte', w_intra, vc)
    den = inter * jnp.einsum('bhtd,bhd->bht', qc, n) + jnp.sum(w_intra, axis=-1)
    h = num / jnp.maximum(jnp.abs(den), jnp.exp(-m_t))[..., None]
    bL = b[..., -1]
    g = bL[..., None] - b + li
    m_next = jnp.maximum(bL + m, jnp.max(g, axis=-1))
    decay = jnp.exp(bL + m - m_next)
    wk = jnp.exp(g - m_next[..., None])
    C_new = decay[..., None, None] * C + jnp.einsum('bhs,bhsd,bhse->bhde', wk, kc, vc)
    n_new = decay[..., None] * n + jnp.einsum('bhs,bhsd->bhd', wk, kc)
    return (C_new, n_new, m_next), h


def mlstm_group(q, k, v, o_pre, i_pre, f_pre, conv_w, conv_b, i_bias, f_bias, norm_g):
    B, S, _ = q.shape
    H, dh = ML_HEADS, ML_HEAD_DIM
    qk = jax.nn.silu(causal_dwconv(jnp.concatenate([q, k], axis=-1), conv_w, conv_b))
    q, k = qk[..., :ML_WIDTH], qk[..., ML_WIDTH:]
    qh = q.reshape(B, S, H, dh).astype(jnp.float32)
    kh = k.reshape(B, S, H, dh).astype(jnp.float32) * (dh ** -0.5)
    vh = v.reshape(B, S, H, dh).astype(jnp.float32)
    log_i = (i_pre + i_bias).astype(jnp.float32)
    log_f = jax.nn.log_sigmoid((f_pre + f_bias).astype(jnp.float32))
    nc = S // ML_CHUNK

    def chunks4(t):
        return t.reshape(B, nc, ML_CHUNK, H, dh).transpose(1, 0, 3, 2, 4)

    def chunks3(t):
        return t.reshape(B, nc, ML_CHUNK, H).transpose(1, 0, 3, 2)

    init = (jnp.zeros((B, H, dh, dh), jnp.float32), jnp.zeros((B, H, dh), jnp.float32),
            jnp.zeros((B, H), jnp.float32))
    _, h = lax.scan(mlstm_chunk_step, init,
                    (chunks4(qh), chunks4(kh), chunks4(vh), chunks3(log_i), chunks3(log_f)))
    h = h.transpose(1, 0, 3, 2, 4).reshape(B, S, H, dh).astype(q.dtype)
    h = rmsnorm(h, norm_g.reshape(H, dh)).reshape(B, S, ML_WIDTH)
    return h * jax.nn.sigmoid(o_pre)


def conv_gated_mlp(x, w_up, conv_w, conv_b, w_down):
    u = causal_dwconv(x @ w_up, conv_w, conv_b)
    g, val = u[..., :D_FF], u[..., D_FF:]
    return (jax.nn.silu(g) * val) @ w_down


def setup_inputs(seed: int = 0) -> dict:
    key = jax.random.key(seed)
    ks = iter(jax.random.split(key, 40))
    f32 = jnp.float32

    def nrm(shape, scale):
        return jax.random.normal(next(ks), shape, f32) * scale

    def gain(shape):
        return 1.0 + nrm(shape, 0.02)

    x = jax.random.normal(next(ks), (BATCH, SEQ, D_MODEL), f32)
    offset = jax.random.randint(next(ks), (BATCH, 1), 0, 1024, dtype=jnp.int32)
    positions = offset + jnp.arange(SEQ, dtype=jnp.int32)[None, :]
    u = jax.random.uniform(next(ks), (DEPTH, RG_WIDTH), f32, 0.9, 0.999) ** (1.0 / RG_C)
    rg_lambda = jnp.log(u) - jnp.log1p(-u)
    ml_f_bias = jnp.linspace(3.0, 6.0, ML_HEADS, dtype=f32)[None, :] + nrm((DEPTH, ML_HEADS), 0.1)
    return {
        'x': x,
        'positions': positions,
        'attn_norm': gain((DEPTH, D_MODEL)),
        'w_in': nrm((DEPTH, D_MODEL, D_IN), D_MODEL ** -0.5),
        'rg_conv_w': nrm((DEPTH, RG_CONV, RG_WIDTH), RG_CONV ** -0.5),
        'rg_conv_b': nrm((DEPTH, RG_WIDTH), 0.01),
        'rg_wa': nrm((DEPTH, RG_BLOCKS, RG_BLOCK, RG_BLOCK), RG_BLOCK ** -0.5),
        'rg_ba': nrm((DEPTH, RG_WIDTH), 0.01),
        'rg_wx': nrm((DEPTH, RG_BLOCKS, RG_BLOCK, RG_BLOCK), RG_BLOCK ** -0.5),
        'rg_bx': nrm((DEPTH, RG_WIDTH), 0.01),
        'rg_lambda': rg_lambda,
        'rg_norm': gain((DEPTH, RG_WIDTH)),
        'da_lambda': nrm((DEPTH, 4, DA_HEAD_DIM), 0.1),
        'da_norm': gain((DEPTH, 2 * DA_HEAD_DIM)),
        'ml_conv_w': nrm((DEPTH, ML_CONV, 2 * ML_WIDTH), ML_CONV ** -0.5),
        'ml_conv_b': nrm((DEPTH, 2 * ML_WIDTH), 0.01),
        'ml_i_bias': nrm((DEPTH, ML_HEADS), 0.1),
        'ml_f_bias': ml_f_bias,
        'ml_norm': gain((DEPTH, ML_WIDTH)),
        'w_out': nrm((DEPTH, D_MIX, D_MODEL), D_MIX ** -0.5),
        'mlp_norm': gain((DEPTH, D_MODEL)),
        'w_up': nrm((DEPTH, D_MODEL, 2 * D_FF), D_MODEL ** -0.5),
        'ffn_conv_w': nrm((DEPTH, FFN_CONV, 2 * D_FF), FFN_CONV ** -0.5),
        'ffn_conv_b': nrm((DEPTH, 2 * D_FF), 0.01),
        'w_down': nrm((DEPTH, D_FF, D_MODEL), D_FF ** -0.5),
        'final_norm': gain((D_MODEL,)),
    }


def reference(x, positions, attn_norm, w_in, rg_conv_w, rg_conv_b, rg_wa, rg_ba, rg_wx, rg_bx,
              rg_lambda, rg_norm, da_lambda, da_norm, ml_conv_w, ml_conv_b, ml_i_bias, ml_f_bias,
              ml_norm, w_out, mlp_norm, w_up, ffn_conv_w, ffn_conv_b, w_down, final_norm):
    cuts = np.cumsum(IN_WIDTHS)[:-1].tolist()
    for l in range(DEPTH):
        lambda_init = 0.8 - 0.6 * math.exp(-0.3 * l)
        h = rmsnorm(x, attn_norm[l])
        z = h @ w_in[l]
        rg_x, rg_g, da_q, da_k, da_v, ml_q, ml_k, ml_v, ml_o, ml_i, ml_f = jnp.split(z, cuts, axis=-1)
        y_rg = rglru_group(rg_x, rg_g, rg_conv_w[l], rg_conv_b[l], rg_wa[l], rg_ba[l], rg_wx[l],
                           rg_bx[l], rg_lambda[l], rg_norm[l])
        y_da = diff_attention_group(da_q, da_k, da_v, positions, da_lambda[l], da_norm[l], lambda_init)
        y_ml = mlstm_group(ml_q, ml_k, ml_v, ml_o, ml_i, ml_f, ml_conv_w[l], ml_conv_b[l],
                           ml_i_bias[l], ml_f_bias[l], ml_norm[l])
        x = x + jnp.concatenate([y_rg, y_da, y_ml], axis=-1) @ w_out[l]
        h = rmsnorm(x, mlp_norm[l])
        x = x + conv_gated_mlp(h, w_up[l], ffn_conv_w[l], ffn_conv_b[l], w_down[l])
    return rmsnorm(x, final_norm)
```

```python
import functools
import math

import numpy as np
import jax
import jax.numpy as jnp
from jax import lax
from jax.experimental import pallas as pl
from jax.experimental.pallas import tpu as pltpu

F32 = jnp.float32
BF16 = jnp.bfloat16

D_MODEL = 1024
RG_WIDTH = 512
RG_BLOCKS = 4
RG_BLOCK = RG_WIDTH // RG_BLOCKS
RG_CONV = 4
RG_C = 8.0
DA_HEADS = 4
DA_HEAD_DIM = 64
DA_WIDTH = DA_HEADS * 2 * DA_HEAD_DIM
ROPE_THETA = 500000.0
ROPE_DIM = DA_HEAD_DIM // 4
NEG_INF = -1e30
ML_HEADS = 4
ML_HEAD_DIM = 128
ML_WIDTH = ML_HEADS * ML_HEAD_DIM
ML_CONV = 4
D_MIX = RG_WIDTH + DA_WIDTH + ML_WIDTH
D_FF = 2816
FFN_CONV = 3
EPS = 1e-6
IN_WIDTHS = (RG_WIDTH, RG_WIDTH, DA_WIDTH, DA_WIDTH, DA_WIDTH,
             ML_WIDTH, ML_WIDTH, ML_WIDTH, ML_WIDTH, ML_HEADS, ML_HEADS)
D_IN = sum(IN_WIDTHS)

LANES = 128
SUBLANES = 8
D_IN_PAD = ((D_IN + LANES - 1) // LANES) * LANES
GATE_COL = (D_IN - 2 * ML_HEADS)
VMEM_LIMIT = 56 * 1024 * 1024

TM_IN = 256
TM_OUT = 512
TT_RG = 256
TQ = 256
TK = 256
L_ML = 256
TM_MLP = 256
CH_IN = 512
CH_FF = 256


def _params(n_axes):
    return pltpu.CompilerParams(dimension_semantics=("arbitrary",) * n_axes,
                                vmem_limit_bytes=VMEM_LIMIT)


def _full(shape):
    nd = len(shape)
    return pl.BlockSpec(shape, lambda *_: (0,) * nd)


def _shift_rows(x, prev8, d):
    rolled = pltpu.roll(x, d, axis=0)
    prolled = pltpu.roll(prev8, d, axis=0)
    row = lax.broadcasted_iota(jnp.int32, prev8.shape, 0)
    top = jnp.where(row < d, prolled, rolled[:SUBLANES])
    return jnp.concatenate([top, rolled[SUBLANES:]], axis=0)


def _causal_conv(x, prev8, w, b):
    k = w.shape[0]
    y = x * w[k - 1:k]
    for d in range(1, k):
        y = y + _shift_rows(x, prev8, d) * w[k - 1 - d:k - d]
    return y + b


def _rms(x, g):
    return x * lax.rsqrt(jnp.mean(x * x, axis=-1, keepdims=True) + EPS) * g


def _inproj_kernel(x_ref, g_ref, w_ref, z_ref):
    h = _rms(x_ref[...], g_ref[...]).astype(BF16)
    n = w_ref.shape[1]
    for c0 in range(0, n, CH_IN):
        cw = min(CH_IN, n - c0)
        z_ref[:, c0:c0 + cw] = jnp.dot(h, w_ref[:, c0:c0 + cw], preferred_element_type=F32)


def _inproj(x2d, g, w_pad):
    t, d = x2d.shape
    n = w_pad.shape[1]
    return pl.pallas_call(
        _inproj_kernel,
        grid=(t // TM_IN,),
        in_specs=[pl.BlockSpec((TM_IN, d), lambda i: (i, 0)), _full((1, d)), _full((d, n))],
        out_specs=pl.BlockSpec((TM_IN, n), lambda i: (i, 0)),
        out_shape=jax.ShapeDtypeStruct((t, n), F32),
        compiler_params=_params(1),
        name="inproj",
    )(x2d, g, w_pad)


def _rglru_kernel(x_ref, gate_ref, cw_ref, cb_ref, wa_ref, ba_ref, wx_ref, bx_ref, lam_ref, ng_ref,
                  y_ref, prev_ref, h_ref):
    @pl.when(pl.program_id(1) == 0)
    def _():
        prev_ref[...] = jnp.zeros_like(prev_ref)
        h_ref[...] = jnp.zeros_like(h_ref)

    x = x_ref[0]
    tt = x.shape[0]
    u = _causal_conv(x, prev_ref[...], cw_ref[...], cb_ref[...])
    prev_ref[...] = x[tt - SUBLANES:]
    ub = u.astype(BF16)
    ra, ri = [], []
    for n in range(RG_BLOCKS):
        un = ub[:, n * RG_BLOCK:(n + 1) * RG_BLOCK]
        ra.append(jnp.dot(un, wa_ref[n], preferred_element_type=F32))
        ri.append(jnp.dot(un, wx_ref[n], preferred_element_type=F32))
    r = jax.nn.sigmoid(jnp.concatenate(ra, axis=-1) + ba_ref[...])
    i = jax.nn.sigmoid(jnp.concatenate(ri, axis=-1) + bx_ref[...])
    log_a = RG_C * r * jax.nn.log_sigmoid(lam_ref[...])
    a = jnp.exp(log_a)
    bt = jnp.sqrt(-jnp.tanh(log_a) * (a * a + 1.0)) * (i * u)
    row = lax.broadcasted_iota(jnp.int32, a.shape, 0)
    d = 1
    while d < tt:
        keep = row >= d
        bt = jnp.where(keep, a * pltpu.roll(bt, d, axis=0) + bt, bt)
        a = jnp.where(keep, a * pltpu.roll(a, d, axis=0), a)
        d *= 2
    h = a * h_ref[SUBLANES - 1:SUBLANES, :] + bt
    h_ref[...] = h[tt - SUBLANES:]
    y = jax.nn.gelu(gate_ref[0]) * h
    y_ref[0] = _rms(y, ng_ref[...]).astype(BF16)


def _rglru(z, cw, cb, wa, ba, wx, bx, lam, ng):
    b, s, _ = z.shape
    w = RG_WIDTH
    return pl.pallas_call(
        _rglru_kernel,
        grid=(b, s // TT_RG),
        in_specs=[pl.BlockSpec((1, TT_RG, w), lambda bi, ti: (bi, ti, 0)),
                  pl.BlockSpec((1, TT_RG, w), lambda bi, ti: (bi, ti, 1)),
                  _full((RG_CONV, w)), _full((1, w)),
                  _full((RG_BLOCKS, RG_BLOCK, RG_BLOCK)), _full((1, w)),
                  _full((RG_BLOCKS, RG_BLOCK, RG_BLOCK)), _full((1, w)),
                  _full((1, w)), _full((1, w))],
        out_specs=pl.BlockSpec((1, TT_RG, w), lambda bi, ti: (bi, ti, 0)),
        out_shape=jax.ShapeDtypeStruct((b, s, w), BF16),
        scratch_shapes=[pltpu.VMEM((SUBLANES, w), F32), pltpu.VMEM((SUBLANES, w), F32)],
        compiler_params=_params(2),
        name="rglru",
    )(z, z, cw, cb, wa, ba, wx, bx, lam, ng)


def _rope_tables(positions):
    half = ROPE_DIM // 2
    inv_freq = ROPE_THETA ** (-jnp.arange(0, ROPE_DIM, 2, dtype=F32) / ROPE_DIM)
    ang = positions.astype(F32)[..., None] * inv_freq
    cos, sin = jnp.cos(ang), jnp.sin(ang)
    ones = jnp.ones(ang.shape[:-1] + (DA_HEAD_DIM - ROPE_DIM,), F32)
    zeros = jnp.zeros_like(ones)
    c_map = jnp.concatenate([cos, cos, ones], axis=-1)
    s_map = jnp.concatenate([-sin, sin, zeros], axis=-1)
    return jnp.concatenate([c_map, c_map], axis=-1), jnp.concatenate([s_map, s_map], axis=-1)


def _rope(x, c, s):
    half = ROPE_DIM // 2
    lane = lax.broadcasted_iota(jnp.int32, (1, LANES), 1)
    first = (lane % DA_HEAD_DIM) < half
    partner = jnp.where(first, pltpu.roll(x, LANES - half, axis=1), pltpu.roll(x, half, axis=1))
    return x * c + partner * s


def _attn_kernel(lam_ref, q_ref, k_ref, v_ref, cq_ref, sq_ref, ck_ref, sk_ref, ng_ref, o_ref,
                 kz_ref, vb_ref, m_ref, l_ref, acc_ref, *, lambda_init):
    qi = pl.program_id(2)
    s_len = k_ref.shape[1]
    lane = lax.broadcasted_iota(jnp.int32, (1, LANES), 1)
    map0 = lane < DA_HEAD_DIM

    @pl.when(qi == 0)
    def _():
        def prep(j, carry):
            rows = pl.ds(pl.multiple_of(j * TK, TK), TK)
            kr = _rope(k_ref[0, rows, :], ck_ref[0, rows, :], sk_ref[0, rows, :])
            kz_ref[0, rows, :] = jnp.where(map0, kr, 0.0).astype(BF16)
            kz_ref[1, rows, :] = jnp.where(map0, 0.0, kr).astype(BF16)
            vb_ref[rows, :] = v_ref[0, rows, :].astype(BF16)
            return carry
        lax.fori_loop(0, s_len // TK, prep, 0)

    scale = DA_HEAD_DIM ** -0.5
    q = (_rope(q_ref[0], cq_ref[0], sq_ref[0]) * scale).astype(BF16)
    m_ref[...] = jnp.full_like(m_ref, NEG_INF)
    l_ref[...] = jnp.zeros_like(l_ref)
    acc_ref[...] = jnp.zeros_like(acc_ref)

    def tile(j, masked):
        rows = pl.ds(pl.multiple_of(j * TK, TK), TK)
        vt = vb_ref[rows, :]
        for c in range(2):
            s = lax.dot_general(q, kz_ref[c, rows, :], (((1,), (1,)), ((), ())),
                                preferred_element_type=F32)
            if masked:
                r_i = lax.broadcasted_iota(jnp.int32, s.shape, 0)
                c_i = lax.broadcasted_iota(jnp.int32, s.shape, 1)
                s = jnp.where(r_i >= c_i, s, NEG_INF)
            m_prev = m_ref[c]
            m_new = jnp.maximum(m_prev, jnp.max(s, axis=-1, keepdims=True))
            alpha = jnp.exp(m_prev - m_new)
            p = jnp.exp(s - m_new)
            l_ref[c] = alpha * l_ref[c] + jnp.sum(p, axis=-1, keepdims=True)
            acc_ref[c] = alpha * acc_ref[c] + jnp.dot(p.astype(BF16), vt, preferred_element_type=F32)
            m_ref[c] = m_new

    def body(j, carry):
        tile(j, False)
        return carry
    lax.fori_loop(0, qi, body, 0)
    tile(qi, True)

    lp = lam_ref[...]
    lam = (jnp.exp(jnp.sum(lp[0:1] * lp[1:2], axis=-1, keepdims=True))
           - jnp.exp(jnp.sum(lp[2:3] * lp[3:4], axis=-1, keepdims=True)) + lambda_init)
    o = acc_ref[0] / l_ref[0] - lam * (acc_ref[1] / l_ref[1])
    o_ref[0] = (_rms(o, ng_ref[...]) * (1.0 - lambda_init)).astype(BF16)


def _attention(z, cos_t, sin_t, lam_p, ng, lambda_init):
    b, s, _ = z.shape
    hw = 2 * DA_HEAD_DIM
    q0 = (2 * RG_WIDTH) // hw
    k0 = q0 + DA_HEADS
    v0 = k0 + DA_HEADS
    return pl.pallas_call(
        functools.partial(_attn_kernel, lambda_init=lambda_init),
        grid=(b, DA_HEADS, s // TQ),
        in_specs=[_full((4, DA_HEAD_DIM)),
                  pl.BlockSpec((1, TQ, hw), lambda bi, hi, qi: (bi, qi, q0 + hi)),
                  pl.BlockSpec((1, s, hw), lambda bi, hi, qi: (bi, 0, k0 + hi)),
                  pl.BlockSpec((1, s, hw), lambda bi, hi, qi: (bi, 0, v0 + hi)),
                  pl.BlockSpec((1, TQ, hw), lambda bi, hi, qi: (bi, qi, 0)),
                  pl.BlockSpec((1, TQ, hw), lambda bi, hi, qi: (bi, qi, 0)),
                  pl.BlockSpec((1, s, hw), lambda bi, hi, qi: (bi, 0, 0)),
                  pl.BlockSpec((1, s, hw), lambda bi, hi, qi: (bi, 0, 0)),
                  _full((1, hw))],
        out_specs=pl.BlockSpec((1, TQ, hw), lambda bi, hi, qi: (bi, qi, hi)),
        out_shape=jax.ShapeDtypeStruct((b, s, DA_WIDTH), BF16),
        scratch_shapes=[pltpu.VMEM((2, s, hw), BF16), pltpu.VMEM((s, hw), BF16),
                        pltpu.VMEM((2, TQ, 1), F32), pltpu.VMEM((2, TQ, 1), F32),
                        pltpu.VMEM((2, TQ, hw), F32)],
        compiler_params=_params(3),
        name="diffattn",
    )(lam_p, z, z, z, cos_t, sin_t, cos_t, sin_t, ng)


def _mlstm_kernel(q_ref, k_ref, v_ref, o_ref, gcol_ref, grow_ref, cw_ref, cb_ref, brow_ref, bcol_ref,
                  ng_ref, y_ref, pq_ref, pk_ref, c_ref, n_ref, m_ref):
    @pl.when(pl.program_id(1) == 0)
    def _():
        pq_ref[...] = jnp.zeros_like(pq_ref)
        pk_ref[...] = jnp.zeros_like(pk_ref)
        c_ref[...] = jnp.zeros_like(c_ref)
        n_ref[...] = jnp.zeros_like(n_ref)
        m_ref[...] = jnp.zeros_like(m_ref)

    w = ML_WIDTH
    dh = ML_HEAD_DIM
    q_raw = q_ref[0]
    k_raw = k_ref[0]
    ln = q_raw.shape[0]
    qs = jax.nn.silu(_causal_conv(q_raw, pq_ref[...], cw_ref[:, :w], cb_ref[:, :w]))
    ks = jax.nn.silu(_causal_conv(k_raw, pk_ref[...], cw_ref[:, w:], cb_ref[:, w:])) * (dh ** -0.5)
    pq_ref[...] = q_raw[ln - SUBLANES:]
    pk_ref[...] = k_raw[ln - SUBLANES:]
    vv = v_ref[0]
    og = jax.nn.sigmoid(o_ref[0])

    gc = gcol_ref[0] + brow_ref[...]
    gr = grow_ref[0] + bcol_ref[...]
    r_i = lax.broadcasted_iota(jnp.int32, (ln, ln), 0)
    c_i = lax.broadcasted_iota(jnp.int32, (ln, ln), 1)
    causal = r_i >= c_i
    b_cols = jnp.dot(causal.astype(F32), jax.nn.log_sigmoid(gc),
                     precision=lax.Precision.HIGHEST, preferred_element_type=F32)
    b_rows = jnp.dot(jax.nn.log_sigmoid(gr), (r_i <= c_i).astype(F32),
                     precision=lax.Precision.HIGHEST, preferred_element_type=F32)

    lane = lax.broadcasted_iota(jnp.int32, (1, LANES), 1)
    sub = lax.broadcasted_iota(jnp.int32, (SUBLANES, 1), 0)
    t_col = lax.broadcasted_iota(jnp.int32, (ln, 1), 0)

    def lane_pick(a, j):
        return jnp.sum(jnp.where(lane == j, a, 0.0), axis=-1, keepdims=True)

    def row_pick(a, j):
        return jnp.sum(jnp.where(sub == j, a, 0.0), axis=0, keepdims=True)

    outs = []
    for hh in range(ML_HEADS):
        sl = slice(hh * dh, (hh + 1) * dh)
        b_col = lane_pick(b_cols, ML_HEADS + hh)
        li_col = lane_pick(gc, hh)
        r_row = row_pick(gr, hh) - row_pick(b_rows, ML_HEADS + hh)
        m_prev = m_ref[hh]
        dmat = jnp.where(causal, b_col + r_row, -jnp.inf)
        m_inter = b_col + m_prev
        m_t = jnp.maximum(m_inter, jnp.max(dmat, axis=-1, keepdims=True))
        qh = qs[:, sl]
        kh = ks[:, sl]
        qb = qh.astype(BF16)
        vb = vv[:, sl].astype(BF16)
        w_intra = lax.dot_general(qb, kh.astype(BF16), (((1,), (1,)), ((), ())),
                                  preferred_element_type=F32) * jnp.exp(dmat - m_t)
        inter = jnp.exp(m_inter - m_t)
        c_prev = c_ref[hh]
        n_prev = n_ref[hh]
        num = (inter * jnp.dot(qb, c_prev.astype(BF16), preferred_element_type=F32)
               + jnp.dot(w_intra.astype(BF16), vb, preferred_element_type=F32))
        den = (inter * jnp.sum(qh * n_prev, axis=-1, keepdims=True)
               + jnp.sum(w_intra, axis=-1, keepdims=True))
        hout = num / jnp.maximum(jnp.abs(den), jnp.exp(-m_t))
        outs.append(_rms(hout, ng_ref[:, sl]))
        b_last = jnp.sum(jnp.where(t_col == ln - 1, b_col, 0.0), axis=0, keepdims=True)
        g_col = b_last - b_col + li_col
        m_next = jnp.maximum(b_last + m_prev, jnp.max(g_col, axis=0, keepdims=True))
        decay = jnp.exp(b_last + m_prev - m_next)
        kw = kh * jnp.exp(g_col - m_next)
        c_ref[hh] = decay * c_prev + lax.dot_general(kw.astype(BF16), vb, (((0,), (0,)), ((), ())),
                                                     preferred_element_type=F32)
        n_ref[hh] = decay * n_prev + jnp.sum(kw, axis=0, keepdims=True)
        m_ref[hh] = m_next
    y_ref[0] = (jnp.concatenate(outs, axis=-1) * og).astype(BF16)


def _mlstm(z, g_rows, cw, cb, brow, bcol, ng):
    b, s, _ = z.shape
    w = ML_WIDTH
    c0 = (2 * RG_WIDTH + 3 * DA_WIDTH) // w
    gate_blk = GATE_COL // LANES
    return pl.pallas_call(
        _mlstm_kernel,
        grid=(b, s // L_ML),
        in_specs=[pl.BlockSpec((1, L_ML, w), lambda bi, ci: (bi, ci, c0)),
                  pl.BlockSpec((1, L_ML, w), lambda bi, ci: (bi, ci, c0 + 1)),
                  pl.BlockSpec((1, L_ML, w), lambda bi, ci: (bi, ci, c0 + 2)),
                  pl.BlockSpec((1, L_ML, w), lambda bi, ci: (bi, ci, c0 + 3)),
                  pl.BlockSpec((1, L_ML, LANES), lambda bi, ci: (bi, ci, gate_blk)),
                  pl.BlockSpec((1, SUBLANES, L_ML), lambda bi, ci: (bi, 0, ci)),
                  _full((ML_CONV, 2 * w)), _full((1, 2 * w)),
                  _full((1, LANES)), _full((SUBLANES, 1)), _full((1, w))],
        out_specs=pl.BlockSpec((1, L_ML, w), lambda bi, ci: (bi, ci, 0)),
        out_shape=jax.ShapeDtypeStruct((b, s, w), BF16),
        scratch_shapes=[pltpu.VMEM((SUBLANES, w), F32), pltpu.VMEM((SUBLANES, w), F32),
                        pltpu.VMEM((ML_HEADS, ML_HEAD_DIM, ML_HEAD_DIM), F32),
                        pltpu.VMEM((ML_HEADS, 1, ML_HEAD_DIM), F32),
                        pltpu.VMEM((ML_HEADS, 1, 1), F32)],
        compiler_params=_params(2),
        name="mlstm",
    )(z, z, z, z, z, g_rows, cw, cb, brow, bcol, ng)


def _outproj_kernel(x_ref, yr_ref, ya_ref, ym_ref, w_ref, g_ref, x1_ref, h_ref):
    acc = x_ref[...]
    acc = acc + jnp.dot(yr_ref[...], w_ref[0:RG_WIDTH, :], preferred_element_type=F32)
    acc = acc + jnp.dot(ya_ref[...], w_ref[RG_WIDTH:RG_WIDTH + DA_WIDTH, :], preferred_element_type=F32)
    acc = acc + jnp.dot(ym_ref[...], w_ref[RG_WIDTH + DA_WIDTH:, :], preferred_element_type=F32)
    x1_ref[...] = acc
    h_ref[...] = _rms(acc, g_ref[...]).astype(BF16)


def _outproj(x2d, y_rg, y_da, y_ml, w_out, g):
    t, d = x2d.shape
    row = lambda wd: pl.BlockSpec((TM_OUT, wd), lambda i: (i, 0))
    return pl.pallas_call(
        _outproj_kernel,
        grid=(t // TM_OUT,),
        in_specs=[row(d), row(RG_WIDTH), row(DA_WIDTH), row(ML_WIDTH), _full((D_MIX, d)), _full((1, d))],
        out_specs=[row(d), row(d)],
        out_shape=[jax.ShapeDtypeStruct((t, d), F32), jax.ShapeDtypeStruct((t, d), BF16)],
        compiler_params=_params(1),
        name="outproj",
    )(x2d, y_rg, y_da, y_ml, w_out, g)


def _mlp_kernel(h_ref, x_ref, wup_ref, cw_ref, cb_ref, wdn_ref, ng_ref, o_ref, tail_ref, *, final_norm):
    @pl.when(pl.program_id(1) == 0)
    def _():
        tail_ref[...] = jnp.zeros_like(tail_ref)

    h = h_ref[0]
    tm = h.shape[0]
    o_ref[0] = x_ref[0]
    for c0 in range(0, D_FF, CH_FF):
        halves = []
        for base in (c0, D_FF + c0):
            cols = slice(base, base + CH_FF)
            u = jnp.dot(h, wup_ref[:, cols], preferred_element_type=F32)
            halves.append(_causal_conv(u, tail_ref[:, cols], cw_ref[:, cols], cb_ref[:, cols]))
            tail_ref[:, cols] = u[tm - SUBLANES:]
        act = (jax.nn.silu(halves[0]) * halves[1]).astype(BF16)
        o_ref[0] += jnp.dot(act, wdn_ref[c0:c0 + CH_FF, :], preferred_element_type=F32)
    if final_norm:
        o_ref[0] = _rms(o_ref[0], ng_ref[...])


def _mlp(h, x1, w_up, cw, cb, w_down, ng, final_norm):
    b, s, d = x1.shape
    tile = pl.BlockSpec((1, TM_MLP, d), lambda bi, ti: (bi, ti, 0))
    return pl.pallas_call(
        functools.partial(_mlp_kernel, final_norm=final_norm),
        grid=(b, s // TM_MLP),
        in_specs=[tile, tile, _full((d, 2 * D_FF)), _full((FFN_CONV, 2 * D_FF)), _full((1, 2 * D_FF)),
                  _full((D_FF, d)), _full((1, d))],
        out_specs=tile,
        out_shape=jax.ShapeDtypeStruct((b, s, d), F32),
        scratch_shapes=[pltpu.VMEM((SUBLANES, 2 * D_FF), F32)],
        compiler_params=_params(2),
        name="mlp",
    )(h, x1, w_up, cw, cb, w_down, ng)


def kernel(x, positions, attn_norm, w_in, rg_conv_w, rg_conv_b, rg_wa, rg_ba, rg_wx, rg_bx, rg_lambda, rg_norm, da_lambda, da_norm, ml_conv_w, ml_conv_b, ml_i_bias, ml_f_bias, ml_norm, w_out, mlp_norm, w_up, ffn_conv_w, ffn_conv_b, w_down, final_norm):
    b, s, d = x.shape
    depth = w_in.shape[0]
    t = b * s
    row = lambda v: v.reshape(1, -1)
    cos_t, sin_t = _rope_tables(positions)
    for l in range(depth):
        lambda_init = 0.8 - 0.6 * math.exp(-0.3 * l)
        w_in_p = jnp.pad(w_in[l], ((0, 0), (0, D_IN_PAD - D_IN))).astype(BF16)
        z = _inproj(x.reshape(t, d), row(attn_norm[l]), w_in_p).reshape(b, s, D_IN_PAD)
        y_rg = _rglru(z, rg_conv_w[l], row(rg_conv_b[l]), rg_wa[l].astype(BF16), row(rg_ba[l]),
                      rg_wx[l].astype(BF16), row(rg_bx[l]), row(rg_lambda[l]), row(rg_norm[l]))
        y_da = _attention(z, cos_t, sin_t, da_lambda[l], row(da_norm[l]), lambda_init)
        gate_bias = jnp.concatenate([ml_i_bias[l], ml_f_bias[l]])
        g_rows = jnp.swapaxes(z[:, :, GATE_COL:GATE_COL + 2 * ML_HEADS], 1, 2)
        y_ml = _mlstm(z, g_rows, ml_conv_w[l], row(ml_conv_b[l]),
                      jnp.pad(gate_bias, (0, LANES - 2 * ML_HEADS)).reshape(1, LANES),
                      gate_bias.reshape(2 * ML_HEADS, 1), row(ml_norm[l]))
        x1, h2 = _outproj(x.reshape(t, d), y_rg.reshape(t, -1), y_da.reshape(t, -1), y_ml.reshape(t, -1),
                          w_out[l].astype(BF16), row(mlp_norm[l]))
        last = l == depth - 1
        x = _mlp(h2.reshape(b, s, d), x1.reshape(b, s, d), w_up[l].astype(BF16), ffn_conv_w[l],
                 row(ffn_conv_b[l]), w_down[l].astype(BF16), row(final_norm), last)
    return x
```

```python
import functools
import math

import jax
import jax.numpy as jnp
from jax import lax
from jax.experimental import pallas as pl
from jax.experimental.pallas import tpu as pltpu

F32 = jnp.float32
BF16 = jnp.bfloat16

D_MODEL = 1024
RG_WIDTH = 512
RG_BLOCKS = 4
RG_BLOCK = RG_WIDTH // RG_BLOCKS
RG_CONV = 4
RG_C = 8.0
DA_HEADS = 4
DA_HEAD_DIM = 64
DA_WIDTH = DA_HEADS * 2 * DA_HEAD_DIM
ROPE_THETA = 500000.0
ROPE_DIM = DA_HEAD_DIM // 4
NEG_INF = -1e30
ML_HEADS = 4
ML_HEAD_DIM = 128
ML_WIDTH = ML_HEADS * ML_HEAD_DIM
ML_CONV = 4
D_MIX = RG_WIDTH + DA_WIDTH + ML_WIDTH
D_FF = 2816
FFN_CONV = 3
EPS = 1e-6
IN_WIDTHS = (RG_WIDTH, RG_WIDTH, DA_WIDTH, DA_WIDTH, DA_WIDTH,
             ML_WIDTH, ML_WIDTH, ML_WIDTH, ML_WIDTH, ML_HEADS, ML_HEADS)
D_IN = sum(IN_WIDTHS)

LANES = 128
SUBLANES = 8
D_MAIN = D_IN - 2 * ML_HEADS
VMEM_LIMIT = 56 * 1024 * 1024

TM_IN = 1024
TN_IN = 1536
TM_OUT = 1024
TT_RG = 256
TQ = 256
TK = 256
L_ML = 256
TM_MLP = 1024
CH_FF = 256


def _params(n_axes):
    return pltpu.CompilerParams(dimension_semantics=("arbitrary",) * n_axes,
                                vmem_limit_bytes=VMEM_LIMIT)


def _full(shape):
    nd = len(shape)
    return pl.BlockSpec(shape, lambda *_: (0,) * nd)


def _shift_rows(x, prev8, d):
    rolled = pltpu.roll(x, d, axis=0)
    prolled = pltpu.roll(prev8, d, axis=0)
    row = lax.broadcasted_iota(jnp.int32, prev8.shape, 0)
    top = jnp.where(row < d, prolled, rolled[:SUBLANES])
    return jnp.concatenate([top, rolled[SUBLANES:]], axis=0)


def _causal_conv(x, prev8, w, b):
    k = w.shape[0]
    y = x * w[k - 1:k]
    for d in range(1, k):
        y = y + _shift_rows(x, prev8, d) * w[k - 1 - d:k - d]
    return y + b


def _rms(x, g):
    return x * lax.rsqrt(jnp.mean(x * x, axis=-1, keepdims=True) + EPS) * g


def _inproj_kernel(x_ref, g_ref, w_ref, wg_ref, z_ref, zg_ref, h_ref):
    @pl.when(pl.program_id(1) == 0)
    def _():
        h = _rms(x_ref[...], g_ref[...]).astype(BF16)
        h_ref[...] = h
        zg_ref[...] = jnp.dot(h, wg_ref[...], preferred_element_type=F32)

    z_ref[...] = jnp.dot(h_ref[...], w_ref[...], preferred_element_type=F32)


def _inproj(x2d, g, w_main, w_gate):
    t, d = x2d.shape
    n = w_main.shape[1]
    return pl.pallas_call(
        _inproj_kernel,
        grid=(t // TM_IN, n // TN_IN),
        in_specs=[pl.BlockSpec((TM_IN, d), lambda i, j: (i, 0)), _full((1, d)),
                  pl.BlockSpec((d, TN_IN), lambda i, j: (0, j)), _full((d, LANES))],
        out_specs=[pl.BlockSpec((TM_IN, TN_IN), lambda i, j: (i, j)),
                   pl.BlockSpec((TM_IN, LANES), lambda i, j: (i, 0))],
        out_shape=[jax.ShapeDtypeStruct((t, n), F32), jax.ShapeDtypeStruct((t, LANES), F32)],
        scratch_shapes=[pltpu.VMEM((TM_IN, d), BF16)],
        compiler_params=_params(2),
        name="inproj",
    )(x2d, g, w_main, w_gate)


def _rglru_kernel(x_ref, gate_ref, cw_ref, cb_ref, wa_ref, ba_ref, wx_ref, bx_ref, lam_ref, ng_ref,
                  y_ref, prev_ref, h_ref):
    @pl.when(pl.program_id(1) == 0)
    def _():
        prev_ref[...] = jnp.zeros_like(prev_ref)
        h_ref[...] = jnp.zeros_like(h_ref)

    x = x_ref[0]
    tt = x.shape[0]
    u = _causal_conv(x, prev_ref[...], cw_ref[...], cb_ref[...])
    prev_ref[...] = x[tt - SUBLANES:]
    ub = u.astype(BF16)
    ra, ri = [], []
    for n in range(RG_BLOCKS):
        un = ub[:, n * RG_BLOCK:(n + 1) * RG_BLOCK]
        ra.append(jnp.dot(un, wa_ref[n], preferred_element_type=F32))
        ri.append(jnp.dot(un, wx_ref[n], preferred_element_type=F32))
    r = jax.nn.sigmoid(jnp.concatenate(ra, axis=-1) + ba_ref[...])
    i = jax.nn.sigmoid(jnp.concatenate(ri, axis=-1) + bx_ref[...])
    log_a = RG_C * r * jax.nn.log_sigmoid(lam_ref[...])
    a = jnp.exp(log_a)
    bt = jnp.sqrt(-jnp.tanh(log_a) * (a * a + 1.0)) * (i * u)
    row = lax.broadcasted_iota(jnp.int32, a.shape, 0)
    d = 1
    while d < tt:
        keep = row >= d
        bt = jnp.where(keep, a * pltpu.roll(bt, d, axis=0) + bt, bt)
        a = jnp.where(keep, a * pltpu.roll(a, d, axis=0), a)
        d *= 2
    h = a * h_ref[SUBLANES - 1:SUBLANES, :] + bt
    h_ref[...] = h[tt - SUBLANES:]
    y = jax.nn.gelu(gate_ref[0]) * h
    y_ref[0] = _rms(y, ng_ref[...]).astype(BF16)


def _rglru(z, cw, cb, wa, ba, wx, bx, lam, ng):
    b, s, _ = z.shape
    w = RG_WIDTH
    return pl.pallas_call(
        _rglru_kernel,
        grid=(b, s // TT_RG),
        in_specs=[pl.BlockSpec((1, TT_RG, w), lambda bi, ti: (bi, ti, 0)),
                  pl.BlockSpec((1, TT_RG, w), lambda bi, ti: (bi, ti, 1)),
                  _full((RG_CONV, w)), _full((1, w)),
                  _full((RG_BLOCKS, RG_BLOCK, RG_BLOCK)), _full((1, w)),
                  _full((RG_BLOCKS, RG_BLOCK, RG_BLOCK)), _full((1, w)),
                  _full((1, w)), _full((1, w))],
        out_specs=pl.BlockSpec((1, TT_RG, w), lambda bi, ti: (bi, ti, 0)),
        out_shape=jax.ShapeDtypeStruct((b, s, w), BF16),
        scratch_shapes=[pltpu.VMEM((SUBLANES, w), F32), pltpu.VMEM((SUBLANES, w), F32)],
        compiler_params=_params(2),
        name="rglru",
    )(z, z, cw, cb, wa, ba, wx, bx, lam, ng)


def _rope_table_kernel(pos_ref, f_ref, c_ref, s_ref):
    ang = pos_ref[0].astype(F32) * f_ref[...]
    cos, sin = jnp.cos(ang), jnp.sin(ang)
    rest = (DA_HEAD_DIM - ROPE_DIM, ang.shape[1])
    c_map = jnp.concatenate([cos, cos, jnp.ones(rest, F32)], axis=0)
    s_map = jnp.concatenate([-sin, sin, jnp.zeros(rest, F32)], axis=0)
    c_ref[0] = jnp.concatenate([c_map, c_map], axis=0).T
    s_ref[0] = jnp.concatenate([s_map, s_map], axis=0).T


def _rope_tables(positions):
    b, s = positions.shape
    half = ROPE_DIM // 2
    inv_freq = ROPE_THETA ** (-jnp.arange(0, ROPE_DIM, 2, dtype=F32) / ROPE_DIM)
    tab = pl.BlockSpec((1, s, 2 * DA_HEAD_DIM), lambda bi: (bi, 0, 0))
    return pl.pallas_call(
        _rope_table_kernel,
        grid=(b,),
        in_specs=[pl.BlockSpec((1, 1, s), lambda bi: (bi, 0, 0)), _full((half, 1))],
        out_specs=[tab, tab],
        out_shape=[jax.ShapeDtypeStruct((b, s, 2 * DA_HEAD_DIM), F32)] * 2,
        compiler_params=_params(1),
        name="rope_tables",
    )(positions.reshape(b, 1, s), inv_freq.reshape(half, 1))


def _rope(x, c, s):
    half = ROPE_DIM // 2
    lane = lax.broadcasted_iota(jnp.int32, (1, LANES), 1)
    first = (lane % DA_HEAD_DIM) < half
    partner = jnp.where(first, pltpu.roll(x, LANES - half, axis=1), pltpu.roll(x, half, axis=1))
    return x * c + partner * s


def _attn_kernel(lam_ref, q_ref, k_ref, v_ref, cos_ref, sin_ref, ng_ref, o_ref,
                 qb_ref, kz_ref, vb_ref, sc_ref, p_ref, *, lambda_init):
    s_len = k_ref.shape[1]
    nt = s_len // TK
    groups = TK // LANES
    lane = lax.broadcasted_iota(jnp.int32, (1, LANES), 1)
    map0 = lane < DA_HEAD_DIM
    q_scale = DA_HEAD_DIM ** -0.5 * math.log2(math.e)

    for j in range(nt):
        rows = slice(j * TK, (j + 1) * TK)
        cos, sin = cos_ref[0, rows, :], sin_ref[0, rows, :]
        kr = _rope(k_ref[0, rows, :], cos, sin)
        kz_ref[0, rows, :] = jnp.where(map0, kr, 0.0).astype(BF16)
        kz_ref[1, rows, :] = jnp.where(map0, 0.0, kr).astype(BF16)
        vb_ref[rows, :] = v_ref[0, rows, :].astype(BF16)
        qb_ref[rows, :] = (_rope(q_ref[0, rows, :], cos, sin) * q_scale).astype(BF16)

    lp = lam_ref[...]
    lam = (jnp.exp(jnp.sum(lp[0:1] * lp[1:2], axis=-1, keepdims=True))
           - jnp.exp(jnp.sum(lp[2:3] * lp[3:4], axis=-1, keepdims=True)) + lambda_init)
    on_or_below_diag = (lax.broadcasted_iota(jnp.int32, (TQ, TK), 0)
                        >= lax.broadcasted_iota(jnp.int32, (TQ, TK), 1))

    for qi in range(nt):
        kv = (qi + 1) * TK
        q = qb_ref[qi * TQ:(qi + 1) * TQ, :]
        for c in range(2):
            sc_ref[c, :, :kv] = lax.dot_general(q, kz_ref[c, :kv, :], (((1,), (1,)), ((), ())),
                                                preferred_element_type=F32)
        row_sums = []
        for c in range(2):
            m = None
            for j in range(qi + 1):
                cols = slice(j * TK, (j + 1) * TK)
                s = sc_ref[c, :, cols]
                if j == qi:
                    s = jnp.where(on_or_below_diag, s, NEG_INF)
                    sc_ref[c, :, cols] = s
                for g in range(groups):
                    sg = s[:, g * LANES:(g + 1) * LANES]
                    m = sg if m is None else jnp.maximum(m, sg)
            m_b = jnp.broadcast_to(jnp.max(m, axis=-1, keepdims=True), (TQ, LANES))
            m_b = jnp.concatenate([m_b] * groups, axis=-1)
            l = jnp.zeros((TQ, LANES), F32)
            for j in range(qi + 1):
                cols = slice(j * TK, (j + 1) * TK)
                p = jnp.exp2(sc_ref[c, :, cols] - m_b)
                for g in range(groups):
                    l = l + p[:, g * LANES:(g + 1) * LANES]
                p_ref[c * TQ:(c + 1) * TQ, cols] = p.astype(BF16)
            row_sums.append(jnp.sum(l, axis=-1, keepdims=True))
        acc = jnp.dot(p_ref[:, :kv], vb_ref[:kv, :], preferred_element_type=F32)
        o = acc[:TQ] / row_sums[0] - lam * (acc[TQ:] / row_sums[1])
        o_ref[0, qi * TQ:(qi + 1) * TQ, :] = (_rms(o, ng_ref[...]) * (1.0 - lambda_init)).astype(BF16)


def _attention(z, cos_t, sin_t, lam_p, ng, lambda_init):
    b, s, _ = z.shape
    hw = 2 * DA_HEAD_DIM
    q0 = (2 * RG_WIDTH) // hw
    k0 = q0 + DA_HEADS
    v0 = k0 + DA_HEADS
    seq = lambda c0: pl.BlockSpec((1, s, hw), lambda bi, hi: (bi, 0, c0 + hi))
    tab = pl.BlockSpec((1, s, hw), lambda bi, hi: (bi, 0, 0))
    return pl.pallas_call(
        functools.partial(_attn_kernel, lambda_init=lambda_init),
        grid=(b, DA_HEADS),
        in_specs=[_full((4, DA_HEAD_DIM)), seq(q0), seq(k0), seq(v0), tab, tab, _full((1, hw))],
        out_specs=pl.BlockSpec((1, s, hw), lambda bi, hi: (bi, 0, hi)),
        out_shape=jax.ShapeDtypeStruct((b, s, DA_WIDTH), BF16),
        scratch_shapes=[pltpu.VMEM((s, hw), BF16), pltpu.VMEM((2, s, hw), BF16), pltpu.VMEM((s, hw), BF16),
                        pltpu.VMEM((2, TQ, s), F32), pltpu.VMEM((2 * TQ, s), BF16)],
        compiler_params=_params(2),
        name="diffattn",
    )(lam_p, z, z, z, cos_t, sin_t, ng)


def _mlstm_kernel(q_ref, k_ref, v_ref, o_ref, gcol_ref, grow_ref, cw_ref, cb_ref, brow_ref, bcol_ref,
                  ng_ref, y_ref, pq_ref, pk_ref, c_ref, n_ref, m_ref):
    @pl.when(pl.program_id(1) == 0)
    def _():
        pq_ref[...] = jnp.zeros_like(pq_ref)
        pk_ref[...] = jnp.zeros_like(pk_ref)
        c_ref[...] = jnp.zeros_like(c_ref)
        n_ref[...] = jnp.zeros_like(n_ref)
        m_ref[...] = jnp.zeros_like(m_ref)

    w = ML_WIDTH
    dh = ML_HEAD_DIM
    q_raw = q_ref[0]
    k_raw = k_ref[0]
    ln = q_raw.shape[0]
    qs = jax.nn.silu(_causal_conv(q_raw, pq_ref[...], cw_ref[:, :w], cb_ref[:, :w]))
    ks = jax.nn.silu(_causal_conv(k_raw, pk_ref[...], cw_ref[:, w:], cb_ref[:, w:])) * (dh ** -0.5)
    pq_ref[...] = q_raw[ln - SUBLANES:]
    pk_ref[...] = k_raw[ln - SUBLANES:]
    vv = v_ref[0]
    og = jax.nn.sigmoid(o_ref[0])

    gc = gcol_ref[0] + brow_ref[...]
    gr = grow_ref[0] + bcol_ref[...]
    r_i = lax.broadcasted_iota(jnp.int32, (ln, ln), 0)
    c_i = lax.broadcasted_iota(jnp.int32, (ln, ln), 1)
    causal = r_i >= c_i
    b_cols = jnp.dot(causal.astype(F32), jax.nn.log_sigmoid(gc),
                     precision=lax.Precision.HIGHEST, preferred_element_type=F32)
    b_rows = jnp.dot(jax.nn.log_sigmoid(gr), (r_i <= c_i).astype(F32),
                     precision=lax.Precision.HIGHEST, preferred_element_type=F32)

    lane = lax.broadcasted_iota(jnp.int32, (1, LANES), 1)
    sub = lax.broadcasted_iota(jnp.int32, (SUBLANES, 1), 0)
    t_col = lax.broadcasted_iota(jnp.int32, (ln, 1), 0)

    def lane_pick(a, j):
        return jnp.sum(jnp.where(lane == j, a, 0.0), axis=-1, keepdims=True)

    def row_pick(a, j):
        return jnp.sum(jnp.where(sub == j, a, 0.0), axis=0, keepdims=True)

    outs = []
    for hh in range(ML_HEADS):
        sl = slice(hh * dh, (hh + 1) * dh)
        b_col = lane_pick(b_cols, ML_HEADS + hh)
        li_col = lane_pick(gc, hh)
        r_row = row_pick(gr, hh) - row_pick(b_rows, ML_HEADS + hh)
        m_prev = m_ref[hh]
        dmat = jnp.where(causal, b_col + r_row, -jnp.inf)
        m_inter = b_col + m_prev
        m_t = jnp.maximum(m_inter, jnp.max(dmat, axis=-1, keepdims=True))
        qh = qs[:, sl]
        kh = ks[:, sl]
        qb = qh.astype(BF16)
        vb = vv[:, sl].astype(BF16)
        w_intra = lax.dot_general(qb, kh.astype(BF16), (((1,), (1,)), ((), ())),
                                  preferred_element_type=F32) * jnp.exp(dmat - m_t)
        inter = jnp.exp(m_inter - m_t)
        c_prev = c_ref[hh]
        n_prev = n_ref[hh]
        num = (inter * jnp.dot(qb, c_prev.astype(BF16), preferred_element_type=F32)
               + jnp.dot(w_intra.astype(BF16), vb, preferred_element_type=F32))
        den = (inter * jnp.sum(qh * n_prev, axis=-1, keepdims=True)
               + jnp.sum(w_intra, axis=-1, keepdims=True))
        hout = num / jnp.maximum(jnp.abs(den), jnp.exp(-m_t))
        outs.append(_rms(hout, ng_ref[:, sl]))
        b_last = jnp.sum(jnp.where(t_col == ln - 1, b_col, 0.0), axis=0, keepdims=True)
        g_col = b_last - b_col + li_col
        m_next = jnp.maximum(b_last + m_prev, jnp.max(g_col, axis=0, keepdims=True))
        decay = jnp.exp(b_last + m_prev - m_next)
        kw = kh * jnp.exp(g_col - m_next)
        c_ref[hh] = decay * c_prev + lax.dot_general(kw.astype(BF16), vb, (((0,), (0,)), ((), ())),
                                                     preferred_element_type=F32)
        n_ref[hh] = decay * n_prev + jnp.sum(kw, axis=0, keepdims=True)
        m_ref[hh] = m_next
    y_ref[0] = (jnp.concatenate(outs, axis=-1) * og).astype(BF16)


def _mlstm(z, zg, g_rows, cw, cb, brow, bcol, ng):
    b, s, _ = z.shape
    w = ML_WIDTH
    c0 = (2 * RG_WIDTH + 3 * DA_WIDTH) // w
    return pl.pallas_call(
        _mlstm_kernel,
        grid=(b, s // L_ML),
        in_specs=[pl.BlockSpec((1, L_ML, w), lambda bi, ci: (bi, ci, c0)),
                  pl.BlockSpec((1, L_ML, w), lambda bi, ci: (bi, ci, c0 + 1)),
                  pl.BlockSpec((1, L_ML, w), lambda bi, ci: (bi, ci, c0 + 2)),
                  pl.BlockSpec((1, L_ML, w), lambda bi, ci: (bi, ci, c0 + 3)),
                  pl.BlockSpec((1, L_ML, LANES), lambda bi, ci: (bi, ci, 0)),
                  pl.BlockSpec((1, SUBLANES, L_ML), lambda bi, ci: (bi, 0, ci)),
                  _full((ML_CONV, 2 * w)), _full((1, 2 * w)),
                  _full((1, LANES)), _full((SUBLANES, 1)), _full((1, w))],
        out_specs=pl.BlockSpec((1, L_ML, w), lambda bi, ci: (bi, ci, 0)),
        out_shape=jax.ShapeDtypeStruct((b, s, w), BF16),
        scratch_shapes=[pltpu.VMEM((SUBLANES, w), F32), pltpu.VMEM((SUBLANES, w), F32),
                        pltpu.VMEM((ML_HEADS, ML_HEAD_DIM, ML_HEAD_DIM), F32),
                        pltpu.VMEM((ML_HEADS, 1, ML_HEAD_DIM), F32),
                        pltpu.VMEM((ML_HEADS, 1, 1), F32)],
        compiler_params=_params(2),
        name="mlstm",
    )(z, z, z, z, zg, g_rows, cw, cb, brow, bcol, ng)


def _outproj_kernel(x_ref, yr_ref, ya_ref, ym_ref, w_ref, g_ref, x1_ref, h_ref):
    acc = x_ref[...]
    acc = acc + jnp.dot(yr_ref[...], w_ref[0:RG_WIDTH, :], preferred_element_type=F32)
    acc = acc + jnp.dot(ya_ref[...], w_ref[RG_WIDTH:RG_WIDTH + DA_WIDTH, :], preferred_element_type=F32)
    acc = acc + jnp.dot(ym_ref[...], w_ref[RG_WIDTH + DA_WIDTH:, :], preferred_element_type=F32)
    x1_ref[...] = acc
    h_ref[...] = _rms(acc, g_ref[...]).astype(BF16)


def _outproj(x2d, y_rg, y_da, y_ml, w_out, g):
    t, d = x2d.shape
    row = lambda wd: pl.BlockSpec((TM_OUT, wd), lambda i: (i, 0))
    return pl.pallas_call(
        _outproj_kernel,
        grid=(t // TM_OUT,),
        in_specs=[row(d), row(RG_WIDTH), row(DA_WIDTH), row(ML_WIDTH), _full((D_MIX, d)), _full((1, d))],
        out_specs=[row(d), row(d)],
        out_shape=[jax.ShapeDtypeStruct((t, d), F32), jax.ShapeDtypeStruct((t, d), BF16)],
        compiler_params=_params(1),
        name="outproj",
    )(x2d, y_rg, y_da, y_ml, w_out, g)


def _mlp_kernel(h_ref, x_ref, wup_ref, cw_ref, cb_ref, wdn_ref, ng_ref, o_ref, tail_ref, act_ref, *, final_norm):
    @pl.when(pl.program_id(1) == 0)
    def _():
        tail_ref[...] = jnp.zeros_like(tail_ref)

    tm = h_ref.shape[1]

    def chunk(j, carry):
        halves = []
        for half in range(2):
            cols = pl.ds(pl.multiple_of(half * D_FF + j * CH_FF, LANES), CH_FF)
            u = jnp.dot(h_ref[0], wup_ref[:, cols], preferred_element_type=F32)
            halves.append(_causal_conv(u, tail_ref[:, cols], cw_ref[:, cols], cb_ref[:, cols]))
            tail_ref[:, cols] = u[tm - SUBLANES:]
        act = jax.nn.silu(halves[0]) * halves[1]
        act_ref[:, pl.ds(pl.multiple_of(j * CH_FF, CH_FF), CH_FF)] = act.astype(BF16)
        return carry
    lax.fori_loop(0, D_FF // CH_FF, chunk, 0)

    y = x_ref[0] + jnp.dot(act_ref[...], wdn_ref[...], preferred_element_type=F32)
    if final_norm:
        y = _rms(y, ng_ref[...])
    o_ref[0] = y


def _mlp(h, x1, w_up, cw, cb, w_down, ng, final_norm):
    b, s, d = x1.shape
    tile = pl.BlockSpec((1, TM_MLP, d), lambda bi, ti: (bi, ti, 0))
    return pl.pallas_call(
        functools.partial(_mlp_kernel, final_norm=final_norm),
        grid=(b, s // TM_MLP),
        in_specs=[tile, tile, _full((d, 2 * D_FF)), _full((FFN_CONV, 2 * D_FF)), _full((1, 2 * D_FF)),
                  _full((D_FF, d)), _full((1, d))],
        out_specs=tile,
        out_shape=jax.ShapeDtypeStruct((b, s, d), F32),
        scratch_shapes=[pltpu.VMEM((SUBLANES, 2 * D_FF), F32), pltpu.VMEM((TM_MLP, D_FF), BF16)],
        compiler_params=_params(2),
        name="mlp",
    )(h, x1, w_up, cw, cb, w_down, ng)


def kernel(x, positions, attn_norm, w_in, rg_conv_w, rg_conv_b, rg_wa, rg_ba, rg_wx, rg_bx, rg_lambda, rg_norm, da_lambda, da_norm, ml_conv_w, ml_conv_b, ml_i_bias, ml_f_bias, ml_norm, w_out, mlp_norm, w_up, ffn_conv_w, ffn_conv_b, w_down, final_norm):
    b, s, d = x.shape
    depth = w_in.shape[0]
    t = b * s
    n_gate = 2 * ML_HEADS
    row = lambda v: v.reshape(1, -1)
    cos_t, sin_t = _rope_tables(positions)
    for l in range(depth):
        lambda_init = 0.8 - 0.6 * math.exp(-0.3 * l)
        w_main = w_in[l, :, :D_MAIN].astype(BF16)
        w_gate = jnp.pad(w_in[l, :, D_MAIN:], ((0, 0), (0, LANES - n_gate))).astype(BF16)
        z, zg = _inproj(x.reshape(t, d), row(attn_norm[l]), w_main, w_gate)
        z = z.reshape(b, s, D_MAIN)
        zg = zg.reshape(b, s, LANES)
        y_rg = _rglru(z, rg_conv_w[l], row(rg_conv_b[l]), rg_wa[l].astype(BF16), row(rg_ba[l]),
                      rg_wx[l].astype(BF16), row(rg_bx[l]), row(rg_lambda[l]), row(rg_norm[l]))
        y_da = _attention(z, cos_t, sin_t, da_lambda[l], row(da_norm[l]), lambda_init)
        gate_bias = jnp.concatenate([ml_i_bias[l], ml_f_bias[l]])
        g_rows = jnp.swapaxes(zg[:, :, :n_gate], 1, 2)
        y_ml = _mlstm(z, zg, g_rows, ml_conv_w[l], row(ml_conv_b[l]),
                      jnp.pad(gate_bias, (0, LANES - n_gate)).reshape(1, LANES),
                      gate_bias.reshape(n_gate, 1), row(ml_norm[l]))
        x1, h2 = _outproj(x.reshape(t, d), y_rg.reshape(t, -1), y_da.reshape(t, -1), y_ml.reshape(t, -1),
                          w_out[l].astype(BF16), row(mlp_norm[l]))
        last = l == depth - 1
        x = _mlp(h2.reshape(b, s, d), x1.reshape(b, s, d), w_up[l].astype(BF16), ffn_conv_w[l],
                 row(ffn_conv_b[l]), w_down[l].astype(BF16), row(final_norm), last)
    return x
```

```python
import functools
import math

import jax
import jax.numpy as jnp
from jax import lax
from jax.experimental import pallas as pl
from jax.experimental.pallas import tpu as pltpu

F32 = jnp.float32
BF16 = jnp.bfloat16

D_MODEL = 1024
RG_WIDTH = 512
RG_BLOCKS = 4
RG_BLOCK = RG_WIDTH // RG_BLOCKS
RG_CONV = 4
RG_C = 8.0
DA_HEADS = 4
DA_HEAD_DIM = 64
DA_WIDTH = DA_HEADS * 2 * DA_HEAD_DIM
ROPE_THETA = 500000.0
ROPE_DIM = DA_HEAD_DIM // 4
NEG_INF = -1e30
ML_HEADS = 4
ML_HEAD_DIM = 128
ML_WIDTH = ML_HEADS * ML_HEAD_DIM
ML_CONV = 4
D_MIX = RG_WIDTH + DA_WIDTH + ML_WIDTH
D_FF = 2816
FFN_CONV = 3
EPS = 1e-6
IN_WIDTHS = (RG_WIDTH, RG_WIDTH, DA_WIDTH, DA_WIDTH, DA_WIDTH,
             ML_WIDTH, ML_WIDTH, ML_WIDTH, ML_WIDTH, ML_HEADS, ML_HEADS)
D_IN = sum(IN_WIDTHS)

LANES = 128
SUBLANES = 8
D_MAIN = D_IN - 2 * ML_HEADS
VMEM_LIMIT = 56 * 1024 * 1024

TM_IN = 1024
TN_IN = 1536
TM_OUT = 1024
TT_RG = 256
SCAN_BLOCK = SUBLANES * SUBLANES
TQ = 256
TK = 256
L_ML = 256
TM_MLP = 1024
CH_FF = 256
RB_FF = 128


def _params(n_axes):
    return pltpu.CompilerParams(dimension_semantics=("arbitrary",) * n_axes,
                                vmem_limit_bytes=VMEM_LIMIT)


def _full(shape):
    nd = len(shape)
    return pl.BlockSpec(shape, lambda *_: (0,) * nd)


def _layer(arr, l):
    nd = arr.ndim
    return pl.BlockSpec((1,) + arr.shape[1:], lambda *_: (l,) + (0,) * (nd - 1),
                        pipeline_mode=pl.Buffered(1))


def _rows3(arr):
    return arr.reshape(arr.shape[0], 1, arr.shape[1])


def _shift_rows(x, prev8, d):
    rolled = pltpu.roll(x, d, axis=0)
    prolled = pltpu.roll(prev8, d, axis=0)
    row = lax.broadcasted_iota(jnp.int32, prev8.shape, 0)
    top = jnp.where(row < d, prolled, rolled[:SUBLANES])
    return jnp.concatenate([top, rolled[SUBLANES:]], axis=0)


def _causal_conv(x, prev8, w, b):
    k = w.shape[0]
    y = x * w[k - 1:k]
    for d in range(1, k):
        y = y + _shift_rows(x, prev8, d) * w[k - 1 - d:k - d]
    return y + b


def _rms(x, g):
    return x * lax.rsqrt(jnp.mean(x * x, axis=-1, keepdims=True) + EPS) * g


def _sigmoid(x):
    return 0.5 + 0.5 * jnp.tanh(0.5 * x)


def _silu(x):
    h = 0.5 * x
    return h * (1.0 + jnp.tanh(h))


def _inproj_kernel(x_ref, g_ref, w_ref, wg_ref, z_ref, zg_ref, h_ref):
    @pl.when(pl.program_id(1) == 0)
    def _():
        h = _rms(x_ref[...], g_ref[0]).astype(BF16)
        h_ref[...] = h
        zg_ref[...] = jnp.dot(h, wg_ref[...], preferred_element_type=F32)

    z_ref[...] = jnp.dot(h_ref[...], w_ref[0], preferred_element_type=F32)


def _inproj(x2d, g, w_in, w_gate, l):
    t, d = x2d.shape
    return pl.pallas_call(
        _inproj_kernel,
        grid=(t // TM_IN, D_MAIN // TN_IN),
        in_specs=[pl.BlockSpec((TM_IN, d), lambda i, j: (i, 0)), _layer(g, l),
                  pl.BlockSpec((1, d, TN_IN), lambda i, j: (l, 0, j)), _full((d, LANES))],
        out_specs=[pl.BlockSpec((TM_IN, TN_IN), lambda i, j: (i, j)),
                   pl.BlockSpec((TM_IN, LANES), lambda i, j: (i, 0))],
        out_shape=[jax.ShapeDtypeStruct((t, D_MAIN), F32), jax.ShapeDtypeStruct((t, LANES), F32)],
        scratch_shapes=[pltpu.VMEM((TM_IN, d), BF16)],
        compiler_params=_params(2),
        name="inproj",
    )(x2d, g, w_in, w_gate)


def _rglru_kernel(x_ref, gate_ref, cw_ref, cb_ref, wa_ref, ba_ref, wx_ref, bx_ref, lam_ref, ng_ref,
                  y_ref, prev_ref, carry_ref, a_ref, b_ref, h_ref):
    @pl.when(pl.program_id(1) == 0)
    def _():
        prev_ref[...] = jnp.zeros_like(prev_ref)
        carry_ref[...] = jnp.zeros_like(carry_ref)

    x = x_ref[0]
    tt = x.shape[0]
    n_slabs = x.shape[1] // LANES
    u = _causal_conv(x, prev_ref[...], cw_ref[0], cb_ref[0])
    prev_ref[...] = x[tt - SUBLANES:]
    ub = u.astype(BF16)
    ra, ri = [], []
    for n in range(RG_BLOCKS):
        un = ub[:, n * RG_BLOCK:(n + 1) * RG_BLOCK]
        ra.append(jnp.dot(un, wa_ref[0, n], preferred_element_type=F32))
        ri.append(jnp.dot(un, wx_ref[0, n], preferred_element_type=F32))
    r = _sigmoid(jnp.concatenate(ra, axis=-1) + ba_ref[0])
    i = _sigmoid(jnp.concatenate(ri, axis=-1) + bx_ref[0])
    log_a = RG_C * r * jax.nn.log_sigmoid(lam_ref[0])
    a = jnp.exp(log_a)
    bt = jnp.sqrt(-jnp.tanh(log_a) * (a * a + 1.0)) * (i * u)
    for s in range(n_slabs):
        a_ref[s] = a[:, s * LANES:(s + 1) * LANES]
        b_ref[s] = bt[:, s * LANES:(s + 1) * LANES]

    sub = lax.broadcasted_iota(jnp.int32, (SUBLANES, LANES), 0)
    for s in range(n_slabs):
        lanes = slice(s * LANES, (s + 1) * LANES)
        carry = jnp.broadcast_to(carry_ref[:, lanes], (SUBLANES, LANES))
        for base in range(0, tt, SCAN_BLOCK):
            acc_a, acc_h = [], []
            for k in range(SUBLANES):
                rows = pl.ds(base + k, SUBLANES, stride=SUBLANES)
                ak, bk = a_ref[s, rows, :], b_ref[s, rows, :]
                acc_h.append(bk if k == 0 else ak * acc_h[-1] + bk)
                acc_a.append(ak if k == 0 else ak * acc_a[-1])
            seg_a, seg_h = acc_a[-1], acc_h[-1]
            for d in (1, 2, 4):
                keep = sub >= d
                seg_h = jnp.where(keep, seg_a * pltpu.roll(seg_h, d, axis=0) + seg_h, seg_h)
                seg_a = jnp.where(keep, seg_a * pltpu.roll(seg_a, d, axis=0), seg_a)
            seg_end = seg_a * carry + seg_h
            enter = jnp.where(sub == 0, carry, pltpu.roll(seg_end, 1, axis=0))
            for k in range(SUBLANES):
                h_ref[s, pl.ds(base + k, SUBLANES, stride=SUBLANES), :] = acc_h[k] + acc_a[k] * enter
            carry = jnp.broadcast_to(seg_end[SUBLANES - 1:, :], (SUBLANES, LANES))
        carry_ref[:, lanes] = carry[:1]

    h = jnp.concatenate([h_ref[s] for s in range(n_slabs)], axis=-1)
    y = jax.nn.gelu(gate_ref[0]) * h
    y_ref[0] = _rms(y, ng_ref[0]).astype(BF16)


def _rglru(z, cw, cb, wa, ba, wx, bx, lam, ng, l):
    b, s, _ = z.shape
    w = RG_WIDTH
    slabs = pltpu.VMEM((w // LANES, TT_RG, LANES), F32)
    return pl.pallas_call(
        _rglru_kernel,
        grid=(b, s // TT_RG),
        in_specs=[pl.BlockSpec((1, TT_RG, w), lambda bi, ti: (bi, ti, 0)),
                  pl.BlockSpec((1, TT_RG, w), lambda bi, ti: (bi, ti, 1)),
                  _layer(cw, l), _layer(cb, l), _layer(wa, l), _layer(ba, l), _layer(wx, l), _layer(bx, l),
                  _layer(lam, l), _layer(ng, l)],
        out_specs=pl.BlockSpec((1, TT_RG, w), lambda bi, ti: (bi, ti, 0)),
        out_shape=jax.ShapeDtypeStruct((b, s, w), BF16),
        scratch_shapes=[pltpu.VMEM((SUBLANES, w), F32), pltpu.VMEM((1, w), F32), slabs, slabs, slabs],
        compiler_params=_params(2),
        name="rglru",
    )(z, z, cw, cb, wa, ba, wx, bx, lam, ng)


def _rope_table_kernel(pos_ref, f_ref, c_ref, s_ref):
    ang = pos_ref[0].astype(F32) * f_ref[...]
    cos, sin = jnp.cos(ang), jnp.sin(ang)
    rest = (DA_HEAD_DIM - ROPE_DIM, ang.shape[1])
    c_map = jnp.concatenate([cos, cos, jnp.ones(rest, F32)], axis=0)
    s_map = jnp.concatenate([-sin, sin, jnp.zeros(rest, F32)], axis=0)
    c_ref[0] = jnp.concatenate([c_map, c_map], axis=0).T
    s_ref[0] = jnp.concatenate([s_map, s_map], axis=0).T


def _rope_tables(positions):
    b, s = positions.shape
    half = ROPE_DIM // 2
    inv_freq = ROPE_THETA ** (-jnp.arange(0, ROPE_DIM, 2, dtype=F32) / ROPE_DIM)
    tab = pl.BlockSpec((1, s, 2 * DA_HEAD_DIM), lambda bi: (bi, 0, 0))
    return pl.pallas_call(
        _rope_table_kernel,
        grid=(b,),
        in_specs=[pl.BlockSpec((1, 1, s), lambda bi: (bi, 0, 0)), _full((half, 1))],
        out_specs=[tab, tab],
        out_shape=[jax.ShapeDtypeStruct((b, s, 2 * DA_HEAD_DIM), F32)] * 2,
        compiler_params=_params(1),
        name="rope_tables",
    )(positions.reshape(b, 1, s), inv_freq.reshape(half, 1))


def _rope(x, c, s):
    half = ROPE_DIM // 2
    lane = lax.broadcasted_iota(jnp.int32, (1, LANES), 1)
    first = (lane % DA_HEAD_DIM) < half
    partner = jnp.where(first, pltpu.roll(x, LANES - half, axis=1), pltpu.roll(x, half, axis=1))
    return x * c + partner * s


def _attn_kernel(lam_ref, q_ref, k_ref, v_ref, cos_ref, sin_ref, ng_ref, o_ref,
                 qb_ref, kz_ref, vb_ref, sc_ref, p_ref, *, lambda_init):
    s_len = k_ref.shape[1]
    nt = s_len // TK
    groups = TK // LANES
    lane = lax.broadcasted_iota(jnp.int32, (1, LANES), 1)
    map0 = lane < DA_HEAD_DIM
    q_scale = DA_HEAD_DIM ** -0.5 * math.log2(math.e)

    for j in range(nt):
        rows = slice(j * TK, (j + 1) * TK)
        cos, sin = cos_ref[0, rows, :], sin_ref[0, rows, :]
        kr = _rope(k_ref[0, rows, :], cos, sin)
        kz_ref[0, rows, :] = jnp.where(map0, kr, 0.0).astype(BF16)
        kz_ref[1, rows, :] = jnp.where(map0, 0.0, kr).astype(BF16)
        vb_ref[rows, :] = v_ref[0, rows, :].astype(BF16)
        qb_ref[rows, :] = (_rope(q_ref[0, rows, :], cos, sin) * q_scale).astype(BF16)

    lp = lam_ref[0]
    lam = (jnp.exp(jnp.sum(lp[0:1] * lp[1:2], axis=-1, keepdims=True))
           - jnp.exp(jnp.sum(lp[2:3] * lp[3:4], axis=-1, keepdims=True)) + lambda_init)
    on_or_below_diag = (lax.broadcasted_iota(jnp.int32, (TQ, TK), 0)
                        >= lax.broadcasted_iota(jnp.int32, (TQ, TK), 1))

    for qi in range(nt):
        kv = (qi + 1) * TK
        q = qb_ref[qi * TQ:(qi + 1) * TQ, :]
        for c in range(2):
            sc_ref[c, :, :kv] = lax.dot_general(q, kz_ref[c, :kv, :], (((1,), (1,)), ((), ())),
                                                preferred_element_type=F32)
        row_sums = []
        for c in range(2):
            m = None
            for j in range(qi + 1):
                cols = slice(j * TK, (j + 1) * TK)
                s = sc_ref[c, :, cols]
                if j == qi:
                    s = jnp.where(on_or_below_diag, s, NEG_INF)
                    sc_ref[c, :, cols] = s
                for g in range(groups):
                    sg = s[:, g * LANES:(g + 1) * LANES]
                    m = sg if m is None else jnp.maximum(m, sg)
            m_b = jnp.broadcast_to(jnp.max(m, axis=-1, keepdims=True), (TQ, LANES))
            m_b = jnp.concatenate([m_b] * groups, axis=-1)
            l = jnp.zeros((TQ, LANES), F32)
            for j in range(qi + 1):
                cols = slice(j * TK, (j + 1) * TK)
                p = jnp.exp2(sc_ref[c, :, cols] - m_b)
                for g in range(groups):
                    l = l + p[:, g * LANES:(g + 1) * LANES]
                p_ref[c * TQ:(c + 1) * TQ, cols] = p.astype(BF16)
            row_sums.append(jnp.sum(l, axis=-1, keepdims=True))
        acc = jnp.dot(p_ref[:, :kv], vb_ref[:kv, :], preferred_element_type=F32)
        o = acc[:TQ] / row_sums[0] - lam * (acc[TQ:] / row_sums[1])
        o_ref[0, qi * TQ:(qi + 1) * TQ, :] = (_rms(o, ng_ref[0]) * (1.0 - lambda_init)).astype(BF16)


def _attention(z, cos_t, sin_t, lam_p, ng, lambda_init, l):
    b, s, _ = z.shape
    hw = 2 * DA_HEAD_DIM
    q0 = (2 * RG_WIDTH) // hw
    k0 = q0 + DA_HEADS
    v0 = k0 + DA_HEADS
    seq = lambda c0: pl.BlockSpec((1, s, hw), lambda bi, hi: (bi, 0, c0 + hi))
    tab = pl.BlockSpec((1, s, hw), lambda bi, hi: (bi, 0, 0))
    return pl.pallas_call(
        functools.partial(_attn_kernel, lambda_init=lambda_init),
        grid=(b, DA_HEADS),
        in_specs=[_layer(lam_p, l), seq(q0), seq(k0), seq(v0), tab, tab, _layer(ng, l)],
        out_specs=pl.BlockSpec((1, s, hw), lambda bi, hi: (bi, 0, hi)),
        out_shape=jax.ShapeDtypeStruct((b, s, DA_WIDTH), BF16),
        scratch_shapes=[pltpu.VMEM((s, hw), BF16), pltpu.VMEM((2, s, hw), BF16), pltpu.VMEM((s, hw), BF16),
                        pltpu.VMEM((2, TQ, s), F32), pltpu.VMEM((2 * TQ, s), BF16)],
        compiler_params=_params(2),
        name="diffattn",
    )(lam_p, z, z, z, cos_t, sin_t, ng)


def _mlstm_kernel(q_ref, k_ref, v_ref, o_ref, gcol_ref, grow_ref, cw_ref, cb_ref, brow_ref, bcol_ref,
                  ng_ref, y_ref, pq_ref, pk_ref, c_ref, n_ref, m_ref):
    @pl.when(pl.program_id(1) == 0)
    def _():
        pq_ref[...] = jnp.zeros_like(pq_ref)
        pk_ref[...] = jnp.zeros_like(pk_ref)
        c_ref[...] = jnp.zeros_like(c_ref)
        n_ref[...] = jnp.zeros_like(n_ref)
        m_ref[...] = jnp.zeros_like(m_ref)

    w = ML_WIDTH
    dh = ML_HEAD_DIM
    q_raw = q_ref[0]
    k_raw = k_ref[0]
    ln = q_raw.shape[0]
    qs = _silu(_causal_conv(q_raw, pq_ref[...], cw_ref[0, :, :w], cb_ref[0, :, :w]))
    ks = _silu(_causal_conv(k_raw, pk_ref[...], cw_ref[0, :, w:], cb_ref[0, :, w:])) * (dh ** -0.5)
    pq_ref[...] = q_raw[ln - SUBLANES:]
    pk_ref[...] = k_raw[ln - SUBLANES:]
    vv = v_ref[0]
    og = _sigmoid(o_ref[0])

    gc = gcol_ref[0] + brow_ref[...]
    gr = grow_ref[0] + bcol_ref[...]
    r_i = lax.broadcasted_iota(jnp.int32, (ln, ln), 0)
    c_i = lax.broadcasted_iota(jnp.int32, (ln, ln), 1)
    causal = r_i >= c_i
    b_cols = jnp.dot(causal.astype(F32), jax.nn.log_sigmoid(gc),
                     precision=lax.Precision.HIGHEST, preferred_element_type=F32)
    b_rows = jnp.dot(jax.nn.log_sigmoid(gr), (r_i <= c_i).astype(F32),
                     precision=lax.Precision.HIGHEST, preferred_element_type=F32)

    lane = lax.broadcasted_iota(jnp.int32, (1, LANES), 1)
    sub = lax.broadcasted_iota(jnp.int32, (SUBLANES, 1), 0)
    t_col = lax.broadcasted_iota(jnp.int32, (ln, 1), 0)

    def lane_pick(a, j):
        return jnp.sum(jnp.where(lane == j, a, 0.0), axis=-1, keepdims=True)

    def row_pick(a, j):
        return jnp.sum(jnp.where(sub == j, a, 0.0), axis=0, keepdims=True)

    outs = []
    for hh in range(ML_HEADS):
        sl = slice(hh * dh, (hh + 1) * dh)
        b_col = lane_pick(b_cols, ML_HEADS + hh)
        li_col = lane_pick(gc, hh)
        r_row = row_pick(gr, hh) - row_pick(b_rows, ML_HEADS + hh)
        m_prev = m_ref[hh]
        dmat = jnp.where(causal, b_col + r_row, -jnp.inf)
        m_inter = b_col + m_prev
        m_t = jnp.maximum(m_inter, jnp.max(dmat, axis=-1, keepdims=True))
        qh = qs[:, sl]
        kh = ks[:, sl]
        qb = qh.astype(BF16)
        vb = vv[:, sl].astype(BF16)
        w_intra = lax.dot_general(qb, kh.astype(BF16), (((1,), (1,)), ((), ())),
                                  preferred_element_type=F32) * jnp.exp(dmat - m_t)
        inter = jnp.exp(m_inter - m_t)
        c_prev = c_ref[hh]
        n_prev = n_ref[hh]
        num = (inter * jnp.dot(qb, c_prev.astype(BF16), preferred_element_type=F32)
               + jnp.dot(w_intra.astype(BF16), vb, preferred_element_type=F32))
        den = (inter * jnp.sum(qh * n_prev, axis=-1, keepdims=True)
               + jnp.sum(w_intra, axis=-1, keepdims=True))
        hout = num / jnp.maximum(jnp.abs(den), jnp.exp(-m_t))
        outs.append(_rms(hout, ng_ref[0, :, sl]))
        b_last = jnp.sum(jnp.where(t_col == ln - 1, b_col, 0.0), axis=0, keepdims=True)
        g_col = b_last - b_col + li_col
        m_next = jnp.maximum(b_last + m_prev, jnp.max(g_col, axis=0, keepdims=True))
        decay = jnp.exp(b_last + m_prev - m_next)
        kw = kh * jnp.exp(g_col - m_next)
        c_ref[hh] = decay * c_prev + lax.dot_general(kw.astype(BF16), vb, (((0,), (0,)), ((), ())),
                                                     preferred_element_type=F32)
        n_ref[hh] = decay * n_prev + jnp.sum(kw, axis=0, keepdims=True)
        m_ref[hh] = m_next
    y_ref[0] = (jnp.concatenate(outs, axis=-1) * og).astype(BF16)


def _mlstm(z, zg, g_rows, cw, cb, brow, bcol, ng, l):
    b, s, _ = z.shape
    w = ML_WIDTH
    c0 = (2 * RG_WIDTH + 3 * DA_WIDTH) // w
    return pl.pallas_call(
        _mlstm_kernel,
        grid=(b, s // L_ML),
        in_specs=[pl.BlockSpec((1, L_ML, w), lambda bi, ci: (bi, ci, c0)),
                  pl.BlockSpec((1, L_ML, w), lambda bi, ci: (bi, ci, c0 + 1)),
                  pl.BlockSpec((1, L_ML, w), lambda bi, ci: (bi, ci, c0 + 2)),
                  pl.BlockSpec((1, L_ML, w), lambda bi, ci: (bi, ci, c0 + 3)),
                  pl.BlockSpec((1, L_ML, LANES), lambda bi, ci: (bi, ci, 0)),
                  pl.BlockSpec((1, SUBLANES, L_ML), lambda bi, ci: (bi, 0, ci)),
                  _layer(cw, l), _layer(cb, l), _full((1, LANES)), _full((SUBLANES, 1)), _layer(ng, l)],
        out_specs=pl.BlockSpec((1, L_ML, w), lambda bi, ci: (bi, ci, 0)),
        out_shape=jax.ShapeDtypeStruct((b, s, w), BF16),
        scratch_shapes=[pltpu.VMEM((SUBLANES, w), F32), pltpu.VMEM((SUBLANES, w), F32),
                        pltpu.VMEM((ML_HEADS, ML_HEAD_DIM, ML_HEAD_DIM), F32),
                        pltpu.VMEM((ML_HEADS, 1, ML_HEAD_DIM), F32),
                        pltpu.VMEM((ML_HEADS, 1, 1), F32)],
        compiler_params=_params(2),
        name="mlstm",
    )(z, z, z, z, zg, g_rows, cw, cb, brow, bcol, ng)


def _outproj_kernel(x_ref, yr_ref, ya_ref, ym_ref, w_ref, g_ref, x1_ref, h_ref):
    acc = x_ref[...]
    acc = acc + jnp.dot(yr_ref[...], w_ref[0, 0:RG_WIDTH, :], preferred_element_type=F32)
    acc = acc + jnp.dot(ya_ref[...], w_ref[0, RG_WIDTH:RG_WIDTH + DA_WIDTH, :], preferred_element_type=F32)
    acc = acc + jnp.dot(ym_ref[...], w_ref[0, RG_WIDTH + DA_WIDTH:, :], preferred_element_type=F32)
    x1_ref[...] = acc
    h_ref[...] = _rms(acc, g_ref[0]).astype(BF16)


def _outproj(x2d, y_rg, y_da, y_ml, w_out, g, l):
    t, d = x2d.shape
    row = lambda wd: pl.BlockSpec((TM_OUT, wd), lambda i: (i, 0))
    return pl.pallas_call(
        _outproj_kernel,
        grid=(t // TM_OUT,),
        in_specs=[row(d), row(RG_WIDTH), row(DA_WIDTH), row(ML_WIDTH), _layer(w_out, l), _layer(g, l)],
        out_specs=[row(d), row(d)],
        out_shape=[jax.ShapeDtypeStruct((t, d), F32), jax.ShapeDtypeStruct((t, d), BF16)],
        compiler_params=_params(1),
        name="outproj",
    )(x2d, y_rg, y_da, y_ml, w_out, g)


def _mlp_kernel(h_ref, x_ref, wup_ref, cw_ref, cb_ref, wdn_ref, ng_ref, o_ref, tail_ref, u_ref, act_ref,
                hs_ref, *, final_norm):
    @pl.when(pl.program_id(1) == 0)
    def _():
        tail_ref[...] = jnp.zeros_like(tail_ref)

    tm = h_ref.shape[1]
    hs_ref[...] = h_ref[0]
    n_ch = D_FF // CH_FF
    assert n_ch % 2 == 1 and n_ch * CH_FF == D_FF

    def cols(half, j):
        return pl.ds(pl.multiple_of(half * D_FF + j * CH_FF, LANES), CH_FF)

    def up(j, slot):
        for half in range(2):
            u_ref[slot, half] = jnp.dot(hs_ref[...], wup_ref[0, :, cols(half, j)], preferred_element_type=F32)

    def gate(j, slot):
        out_cols = pl.ds(pl.multiple_of(j * CH_FF, CH_FF), CH_FF)
        taps = [(cw_ref[0, :, cols(half, j)] * sc, cb_ref[0, :, cols(half, j)] * sc) for half, sc in ((0, 0.5), (1, 1.0))]
        for r0 in range(0, tm, RB_FF):
            halves = []
            for half in range(2):
                prev = tail_ref[:, cols(half, j)] if r0 == 0 else u_ref[slot, half, r0 - SUBLANES:r0, :]
                halves.append(_causal_conv(u_ref[slot, half, r0:r0 + RB_FF, :], prev, *taps[half]))
            g_half = halves[0]
            act_ref[r0:r0 + RB_FF, out_cols] = (g_half * (1.0 + jnp.tanh(g_half)) * halves[1]).astype(BF16)
        for half in range(2):
            tail_ref[:, cols(half, j)] = u_ref[slot, half, tm - SUBLANES:, :]

    up(0, 0)

    def pair(i, carry):
        j = 2 * i
        up(j + 1, 1)
        gate(j, 0)
        up(j + 2, 0)
        gate(j + 1, 1)
        return carry
    lax.fori_loop(0, n_ch // 2, pair, 0)
    gate(n_ch - 1, 0)

    y = x_ref[0] + jnp.dot(act_ref[...], wdn_ref[0], preferred_element_type=F32)
    if final_norm:
        y = _rms(y, ng_ref[...])
    o_ref[0] = y


def _mlp(h, x1, w_up, cw, cb, w_down, ng, final_norm, l):
    b, s, d = x1.shape
    tile = pl.BlockSpec((1, TM_MLP, d), lambda bi, ti: (bi, ti, 0))
    return pl.pallas_call(
        functools.partial(_mlp_kernel, final_norm=final_norm),
        grid=(b, s // TM_MLP),
        in_specs=[tile, tile, _layer(w_up, l), _layer(cw, l), _layer(cb, l), _layer(w_down, l), _full((1, d))],
        out_specs=tile,
        out_shape=jax.ShapeDtypeStruct((b, s, d), F32),
        scratch_shapes=[pltpu.VMEM((SUBLANES, 2 * D_FF), F32), pltpu.VMEM((2, 2, TM_MLP, CH_FF), F32),
                        pltpu.VMEM((TM_MLP, D_FF), BF16), pltpu.VMEM((TM_MLP, d), BF16)],
        compiler_params=_params(2),
        name="mlp",
    )(h, x1, w_up, cw, cb, w_down, ng)


def kernel(x, positions, attn_norm, w_in, rg_conv_w, rg_conv_b, rg_wa, rg_ba, rg_wx, rg_bx, rg_lambda, rg_norm, da_lambda, da_norm, ml_conv_w, ml_conv_b, ml_i_bias, ml_f_bias, ml_norm, w_out, mlp_norm, w_up, ffn_conv_w, ffn_conv_b, w_down, final_norm):
    b, s, d = x.shape
    depth = w_in.shape[0]
    t = b * s
    n_gate = 2 * ML_HEADS
    cos_t, sin_t = _rope_tables(positions)
    w_in_b, w_out_b, w_up_b, w_down_b = (w.astype(BF16) for w in (w_in, w_out, w_up, w_down))
    wa_b, wx_b = rg_wa.astype(BF16), rg_wx.astype(BF16)
    w_gate_b = jnp.pad(w_in[:, :, D_MAIN:], ((0, 0), (0, 0), (0, LANES - n_gate))).astype(BF16)
    gate_bias = jnp.concatenate([ml_i_bias, ml_f_bias], axis=-1)
    attn_norm3, mlp_norm3, rg_norm3, da_norm3, ml_norm3 = map(_rows3, (attn_norm, mlp_norm, rg_norm, da_norm, ml_norm))
    rg_cb3, rg_ba3, rg_bx3, rg_lam3, ml_cb3, ffn_cb3 = map(_rows3, (rg_conv_b, rg_ba, rg_bx, rg_lambda, ml_conv_b, ffn_conv_b))
    for l in range(depth):
        lambda_init = 0.8 - 0.6 * math.exp(-0.3 * l)
        z, zg = _inproj(x.reshape(t, d), attn_norm3, w_in_b, w_gate_b[l], l)
        z = z.reshape(b, s, D_MAIN)
        zg = zg.reshape(b, s, LANES)
        y_rg = _rglru(z, rg_conv_w, rg_cb3, wa_b, rg_ba3, wx_b, rg_bx3, rg_lam3, rg_norm3, l)
        y_da = _attention(z, cos_t, sin_t, da_lambda, da_norm3, lambda_init, l)
        g_rows = jnp.swapaxes(zg[:, :, :n_gate], 1, 2)
        y_ml = _mlstm(z, zg, g_rows, ml_conv_w, ml_cb3,
                      jnp.pad(gate_bias[l], (0, LANES - n_gate)).reshape(1, LANES),
                      gate_bias[l].reshape(n_gate, 1), ml_norm3, l)
        x1, h2 = _outproj(x.reshape(t, d), y_rg.reshape(t, -1), y_da.reshape(t, -1), y_ml.reshape(t, -1),
                          w_out_b, mlp_norm3, l)
        x = _mlp(h2.reshape(b, s, d), x1.reshape(b, s, d), w_up_b, ffn_conv_w, ffn_cb3, w_down_b,
                 final_norm.reshape(1, d), l == depth - 1, l)
    return x
```

```python
import functools
import math

import jax
import jax.numpy as jnp
from jax import lax
from jax.experimental import pallas as pl
from jax.experimental.pallas import tpu as pltpu

F32 = jnp.float32
BF16 = jnp.bfloat16

D_MODEL = 1024
RG_WIDTH = 512
RG_BLOCKS = 4
RG_BLOCK = RG_WIDTH // RG_BLOCKS
RG_CONV = 4
RG_C = 8.0
DA_HEADS = 4
DA_HEAD_DIM = 64
DA_WIDTH = DA_HEADS * 2 * DA_HEAD_DIM
ROPE_THETA = 500000.0
ROPE_DIM = DA_HEAD_DIM // 4
NEG_INF = -1e30
ML_HEADS = 4
ML_HEAD_DIM = 128
ML_WIDTH = ML_HEADS * ML_HEAD_DIM
ML_CONV = 4
D_MIX = RG_WIDTH + DA_WIDTH + ML_WIDTH
D_FF = 2816
FFN_CONV = 3
EPS = 1e-6
IN_WIDTHS = (RG_WIDTH, RG_WIDTH, DA_WIDTH, DA_WIDTH, DA_WIDTH,
             ML_WIDTH, ML_WIDTH, ML_WIDTH, ML_WIDTH, ML_HEADS, ML_HEADS)
D_IN = sum(IN_WIDTHS)

LANES = 128
SUBLANES = 8
D_MAIN = D_IN - 2 * ML_HEADS
VMEM_LIMIT = 56 * 1024 * 1024

TM_IN = 512
TN_IN = 1536
TM_OUT = 1024
TT_RG = 256
SCAN_BLOCK = SUBLANES * SUBLANES
TQ = 256
TK = 256
L_ML = 256
T_ML = L_ML
TM_MLP = 1024
CH_FF = 256
RB_FF = 128


def _params(n_axes):
    return pltpu.CompilerParams(dimension_semantics=("arbitrary",) * n_axes,
                                vmem_limit_bytes=VMEM_LIMIT)


def _full(shape):
    nd = len(shape)
    return pl.BlockSpec(shape, lambda *_: (0,) * nd)


def _layer(arr, l):
    nd = arr.ndim
    return pl.BlockSpec((1,) + arr.shape[1:], lambda *_: (l,) + (0,) * (nd - 1),
                        pipeline_mode=pl.Buffered(1))


def _rows3(arr):
    return arr.reshape(arr.shape[0], 1, arr.shape[1])


def _shift_rows(x, prev8, d):
    rolled = pltpu.roll(x, d, axis=0)
    prolled = pltpu.roll(prev8, d, axis=0)
    row = lax.broadcasted_iota(jnp.int32, prev8.shape, 0)
    top = jnp.where(row < d, prolled, rolled[:SUBLANES])
    return jnp.concatenate([top, rolled[SUBLANES:]], axis=0)


def _causal_conv(x, prev8, w, b):
    k = w.shape[0]
    y = x * w[k - 1:k]
    for d in range(1, k):
        y = y + _shift_rows(x, prev8, d) * w[k - 1 - d:k - d]
    return y + b


def _rms(x, g):
    return x * lax.rsqrt(jnp.mean(x * x, axis=-1, keepdims=True) + EPS) * g


def _sigmoid(x):
    return 0.5 + 0.5 * jnp.tanh(0.5 * x)


def _silu(x):
    h = 0.5 * x
    return h * (1.0 + jnp.tanh(h))


def _inproj_kernel(x_ref, xn_ref, g_ref, w_ref, wg_ref, z_ref, zg_ref, ha_ref, hb_ref):
    half = TM_IN // 2
    g = g_ref[0]

    @pl.when(pl.program_id(0) == 0)
    def _():
        ha_ref[...] = _rms(x_ref[:half], g).astype(BF16)

    def project(h, rows):
        for c0 in range(0, D_MAIN, TN_IN):
            z_ref[rows, c0:c0 + TN_IN] = jnp.dot(h, w_ref[0, :, c0:c0 + TN_IN], preferred_element_type=F32)
        zg_ref[rows, :] = jnp.dot(h, wg_ref[...], preferred_element_type=F32)

    hb_ref[...] = _rms(x_ref[half:], g).astype(BF16)
    project(ha_ref[...], slice(0, half))
    ha_next = _rms(xn_ref[...], g).astype(BF16)
    project(hb_ref[...], slice(half, TM_IN))
    ha_ref[...] = ha_next


def _inproj(x2d, g, w_in, w_gate, l):
    t, d = x2d.shape
    half = TM_IN // 2
    last_half = t // half - 1
    return pl.pallas_call(
        _inproj_kernel,
        grid=(t // TM_IN,),
        in_specs=[pl.BlockSpec((TM_IN, d), lambda i: (i, 0)),
                  pl.BlockSpec((half, d), lambda i: (jnp.minimum(2 * i + 2, last_half), 0)),
                  _layer(g, l), _layer(w_in, l), _full((d, LANES))],
        out_specs=[pl.BlockSpec((TM_IN, D_MAIN), lambda i: (i, 0)),
                   pl.BlockSpec((TM_IN, LANES), lambda i: (i, 0))],
        out_shape=[jax.ShapeDtypeStruct((t, D_MAIN), F32), jax.ShapeDtypeStruct((t, LANES), F32)],
        scratch_shapes=[pltpu.VMEM((half, d), BF16), pltpu.VMEM((half, d), BF16)],
        compiler_params=_params(1),
        name="inproj",
    )(x2d, x2d, g, w_in, w_gate)


def _rglru_kernel(x_ref, gate_ref, cw_ref, cb_ref, wa_ref, ba_ref, wx_ref, bx_ref, lam_ref, ng_ref,
                  y_ref, prev_ref, carry_ref, a_ref, b_ref, h_ref):
    @pl.when(pl.program_id(1) == 0)
    def _():
        prev_ref[...] = jnp.zeros_like(prev_ref)
        carry_ref[...] = jnp.zeros_like(carry_ref)

    x = x_ref[0]
    tt = x.shape[0]
    n_slabs = x.shape[1] // LANES
    u = _causal_conv(x, prev_ref[...], cw_ref[0], cb_ref[0])
    prev_ref[...] = x[tt - SUBLANES:]
    ub = u.astype(BF16)
    ra, ri = [], []
    for n in range(RG_BLOCKS):
        un = ub[:, n * RG_BLOCK:(n + 1) * RG_BLOCK]
        ra.append(jnp.dot(un, wa_ref[0, n], preferred_element_type=F32))
        ri.append(jnp.dot(un, wx_ref[0, n], preferred_element_type=F32))
    r = _sigmoid(jnp.concatenate(ra, axis=-1) + ba_ref[0])
    i = _sigmoid(jnp.concatenate(ri, axis=-1) + bx_ref[0])
    log_a = RG_C * r * jax.nn.log_sigmoid(lam_ref[0])
    a = jnp.exp(log_a)
    bt = jnp.sqrt(-jnp.tanh(log_a) * (a * a + 1.0)) * (i * u)
    for s in range(n_slabs):
        a_ref[s] = a[:, s * LANES:(s + 1) * LANES]
        b_ref[s] = bt[:, s * LANES:(s + 1) * LANES]

    sub = lax.broadcasted_iota(jnp.int32, (SUBLANES, LANES), 0)
    for s in range(n_slabs):
        lanes = slice(s * LANES, (s + 1) * LANES)
        carry = jnp.broadcast_to(carry_ref[:, lanes], (SUBLANES, LANES))
        for base in range(0, tt, SCAN_BLOCK):
            acc_a, acc_h = [], []
            for k in range(SUBLANES):
                rows = pl.ds(base + k, SUBLANES, stride=SUBLANES)
                ak, bk = a_ref[s, rows, :], b_ref[s, rows, :]
                acc_h.append(bk if k == 0 else ak * acc_h[-1] + bk)
                acc_a.append(ak if k == 0 else ak * acc_a[-1])
            seg_a, seg_h = acc_a[-1], acc_h[-1]
            for d in (1, 2, 4):
                keep = sub >= d
                seg_h = jnp.where(keep, seg_a * pltpu.roll(seg_h, d, axis=0) + seg_h, seg_h)
                seg_a = jnp.where(keep, seg_a * pltpu.roll(seg_a, d, axis=0), seg_a)
            seg_end = seg_a * carry + seg_h
            enter = jnp.where(sub == 0, carry, pltpu.roll(seg_end, 1, axis=0))
            for k in range(SUBLANES):
                h_ref[s, pl.ds(base + k, SUBLANES, stride=SUBLANES), :] = acc_h[k] + acc_a[k] * enter
            carry = jnp.broadcast_to(seg_end[SUBLANES - 1:, :], (SUBLANES, LANES))
        carry_ref[:, lanes] = carry[:1]

    h = jnp.concatenate([h_ref[s] for s in range(n_slabs)], axis=-1)
    y = jax.nn.gelu(gate_ref[0]) * h
    y_ref[0] = _rms(y, ng_ref[0]).astype(BF16)


def _rglru(z, cw, cb, wa, ba, wx, bx, lam, ng, l):
    b, s, _ = z.shape
    w = RG_WIDTH
    slabs = pltpu.VMEM((w // LANES, TT_RG, LANES), F32)
    return pl.pallas_call(
        _rglru_kernel,
        grid=(b, s // TT_RG),
        in_specs=[pl.BlockSpec((1, TT_RG, w), lambda bi, ti: (bi, ti, 0)),
                  pl.BlockSpec((1, TT_RG, w), lambda bi, ti: (bi, ti, 1)),
                  _layer(cw, l), _layer(cb, l), _layer(wa, l), _layer(ba, l), _layer(wx, l), _layer(bx, l),
                  _layer(lam, l), _layer(ng, l)],
        out_specs=pl.BlockSpec((1, TT_RG, w), lambda bi, ti: (bi, ti, 0)),
        out_shape=jax.ShapeDtypeStruct((b, s, w), BF16),
        scratch_shapes=[pltpu.VMEM((SUBLANES, w), F32), pltpu.VMEM((1, w), F32), slabs, slabs, slabs],
        compiler_params=_params(2),
        name="rglru",
    )(z, z, cw, cb, wa, ba, wx, bx, lam, ng)


def _rope_table_kernel(pos_ref, f_ref, c_ref, s_ref):
    ang = pos_ref[0].astype(F32) * f_ref[...]
    cos, sin = jnp.cos(ang), jnp.sin(ang)
    rest = (DA_HEAD_DIM - ROPE_DIM, ang.shape[1])
    c_map = jnp.concatenate([cos, cos, jnp.ones(rest, F32)], axis=0)
    s_map = jnp.concatenate([-sin, sin, jnp.zeros(rest, F32)], axis=0)
    c_ref[0] = jnp.concatenate([c_map, c_map], axis=0).T
    s_ref[0] = jnp.concatenate([s_map, s_map], axis=0).T


def _rope_tables(positions):
    b, s = positions.shape
    half = ROPE_DIM // 2
    inv_freq = ROPE_THETA ** (-jnp.arange(0, ROPE_DIM, 2, dtype=F32) / ROPE_DIM)
    tab = pl.BlockSpec((1, s, 2 * DA_HEAD_DIM), lambda bi: (bi, 0, 0))
    return pl.pallas_call(
        _rope_table_kernel,
        grid=(b,),
        in_specs=[pl.BlockSpec((1, 1, s), lambda bi: (bi, 0, 0)), _full((half, 1))],
        out_specs=[tab, tab],
        out_shape=[jax.ShapeDtypeStruct((b, s, 2 * DA_HEAD_DIM), F32)] * 2,
        compiler_params=_params(1),
        name="rope_tables",
    )(positions.reshape(b, 1, s), inv_freq.reshape(half, 1))


def _rope(x, c, s):
    half = ROPE_DIM // 2
    lane = lax.broadcasted_iota(jnp.int32, (1, LANES), 1)
    first = (lane % DA_HEAD_DIM) < half
    partner = jnp.where(first, pltpu.roll(x, LANES - half, axis=1), pltpu.roll(x, half, axis=1))
    return x * c + partner * s


def _attn_kernel(lam_ref, q_ref, k_ref, v_ref, cos_ref, sin_ref, ng_ref, o_ref,
                 qb_ref, kz_ref, vb_ref, sc_ref, p_ref, *, lambda_init):
    s_len = k_ref.shape[1]
    nt = s_len // TK
    groups = TK // LANES
    lane = lax.broadcasted_iota(jnp.int32, (1, LANES), 1)
    map0 = lane < DA_HEAD_DIM
    q_scale = DA_HEAD_DIM ** -0.5 * math.log2(math.e)

    for j in range(nt):
        rows = slice(j * TK, (j + 1) * TK)
        cos, sin = cos_ref[0, rows, :], sin_ref[0, rows, :]
        kr = _rope(k_ref[0, rows, :], cos, sin)
        kz_ref[0, rows, :] = jnp.where(map0, kr, 0.0).astype(BF16)
        kz_ref[1, rows, :] = jnp.where(map0, 0.0, kr).astype(BF16)
        vb_ref[rows, :] = v_ref[0, rows, :].astype(BF16)
        qb_ref[rows, :] = (_rope(q_ref[0, rows, :], cos, sin) * q_scale).astype(BF16)

    lp = lam_ref[0]
    lam = (jnp.exp(jnp.sum(lp[0:1] * lp[1:2], axis=-1, keepdims=True))
           - jnp.exp(jnp.sum(lp[2:3] * lp[3:4], axis=-1, keepdims=True)) + lambda_init)
    on_or_below_diag = (lax.broadcasted_iota(jnp.int32, (TQ, TK), 0)
                        >= lax.broadcasted_iota(jnp.int32, (TQ, TK), 1))

    for qi in range(nt):
        kv = (qi + 1) * TK
        q = qb_ref[qi * TQ:(qi + 1) * TQ, :]
        for c in range(2):
            sc_ref[c, :, :kv] = lax.dot_general(q, kz_ref[c, :kv, :], (((1,), (1,)), ((), ())),
                                                preferred_element_type=F32)
        row_sums = []
        for c in range(2):
            m = None
            for j in range(qi + 1):
                cols = slice(j * TK, (j + 1) * TK)
                s = sc_ref[c, :, cols]
                if j == qi:
                    s = jnp.where(on_or_below_diag, s, NEG_INF)
                    sc_ref[c, :, cols] = s
                for g in range(groups):
                    sg = s[:, g * LANES:(g + 1) * LANES]
                    m = sg if m is None else jnp.maximum(m, sg)
            m_b = jnp.broadcast_to(jnp.max(m, axis=-1, keepdims=True), (TQ, LANES))
            m_b = jnp.concatenate([m_b] * groups, axis=-1)
            l = jnp.zeros((TQ, LANES), F32)
            for j in range(qi + 1):
                cols = slice(j * TK, (j + 1) * TK)
                p = jnp.exp2(sc_ref[c, :, cols] - m_b)
                for g in range(groups):
                    l = l + p[:, g * LANES:(g + 1) * LANES]
                p_ref[c * TQ:(c + 1) * TQ, cols] = p.astype(BF16)
            row_sums.append(jnp.sum(l, axis=-1, keepdims=True))
        acc = jnp.dot(p_ref[:, :kv], vb_ref[:kv, :], preferred_element_type=F32)
        o = acc[:TQ] / row_sums[0] - lam * (acc[TQ:] / row_sums[1])
        o_ref[0, qi * TQ:(qi + 1) * TQ, :] = (_rms(o, ng_ref[0]) * (1.0 - lambda_init)).astype(BF16)


def _attention(z, cos_t, sin_t, lam_p, ng, lambda_init, l):
    b, s, _ = z.shape
    hw = 2 * DA_HEAD_DIM
    q0 = (2 * RG_WIDTH) // hw
    k0 = q0 + DA_HEADS
    v0 = k0 + DA_HEADS
    seq = lambda c0: pl.BlockSpec((1, s, hw), lambda bi, hi: (bi, 0, c0 + hi))
    tab = pl.BlockSpec((1, s, hw), lambda bi, hi: (bi, 0, 0))
    return pl.pallas_call(
        functools.partial(_attn_kernel, lambda_init=lambda_init),
        grid=(b, DA_HEADS),
        in_specs=[_layer(lam_p, l), seq(q0), seq(k0), seq(v0), tab, tab, _layer(ng, l)],
        out_specs=pl.BlockSpec((1, s, hw), lambda bi, hi: (bi, 0, hi)),
        out_shape=jax.ShapeDtypeStruct((b, s, DA_WIDTH), BF16),
        scratch_shapes=[pltpu.VMEM((s, hw), BF16), pltpu.VMEM((2, s, hw), BF16), pltpu.VMEM((s, hw), BF16),
                        pltpu.VMEM((2, TQ, s), F32), pltpu.VMEM((2 * TQ, s), BF16)],
        compiler_params=_params(2),
        name="diffattn",
    )(lam_p, z, z, z, cos_t, sin_t, ng)


def _mlstm_kernel(q_ref, k_ref, v_ref, o_ref, gcol_ref, grow_ref, cw_ref, cb_ref, brow_ref, bcol_ref,
                  ng_ref, y_ref, pq_ref, pk_ref, c_ref, n_ref, m_ref):
    @pl.when(pl.program_id(1) == 0)
    def _():
        pq_ref[...] = jnp.zeros_like(pq_ref)
        pk_ref[...] = jnp.zeros_like(pk_ref)
        c_ref[...] = jnp.zeros_like(c_ref)
        n_ref[...] = jnp.zeros_like(n_ref)
        m_ref[...] = jnp.zeros_like(m_ref)

    rows_total = q_ref.shape[1]
    for r0 in range(0, rows_total, L_ML):
        rows = slice(r0, r0 + L_ML)
        before = slice(r0 - SUBLANES, r0)
        pq = pq_ref[...] if r0 == 0 else q_ref[0, before, :]
        pk = pk_ref[...] if r0 == 0 else k_ref[0, before, :]
        y_ref[0, rows, :] = _mlstm_chunk(q_ref[0, rows, :], k_ref[0, rows, :], pq, pk, v_ref[0, rows, :],
                                         o_ref[0, rows, :], gcol_ref[0, rows, :], grow_ref[0, :, rows],
                                         cw_ref, cb_ref, brow_ref, bcol_ref, ng_ref, c_ref, n_ref, m_ref)
    pq_ref[...] = q_ref[0, rows_total - SUBLANES:, :]
    pk_ref[...] = k_ref[0, rows_total - SUBLANES:, :]


def _mlstm_chunk(q_raw, k_raw, pq, pk, vv, o_pre, gcol, grow, cw_ref, cb_ref, brow_ref, bcol_ref, ng_ref,
                 c_ref, n_ref, m_ref):
    w = ML_WIDTH
    dh = ML_HEAD_DIM
    ln = q_raw.shape[0]
    qs = _silu(_causal_conv(q_raw, pq, cw_ref[0, :, :w], cb_ref[0, :, :w]))
    ks = _silu(_causal_conv(k_raw, pk, cw_ref[0, :, w:], cb_ref[0, :, w:])) * (dh ** -0.5)
    og = _sigmoid(o_pre)

    gc = gcol + brow_ref[...]
    gr = grow + bcol_ref[...]
    r_i = lax.broadcasted_iota(jnp.int32, (ln, ln), 0)
    c_i = lax.broadcasted_iota(jnp.int32, (ln, ln), 1)
    causal = r_i >= c_i
    b_cols = jnp.dot(causal.astype(F32), jax.nn.log_sigmoid(gc),
                     precision=lax.Precision.HIGHEST, preferred_element_type=F32)
    b_rows = jnp.dot(jax.nn.log_sigmoid(gr), (r_i <= c_i).astype(F32),
                     precision=lax.Precision.HIGHEST, preferred_element_type=F32)

    lane = lax.broadcasted_iota(jnp.int32, (1, LANES), 1)
    sub = lax.broadcasted_iota(jnp.int32, (SUBLANES, 1), 0)
    t_col = lax.broadcasted_iota(jnp.int32, (ln, 1), 0)

    def lane_pick(a, j):
        return jnp.sum(jnp.where(lane == j, a, 0.0), axis=-1, keepdims=True)

    def row_pick(a, j):
        return jnp.sum(jnp.where(sub == j, a, 0.0), axis=0, keepdims=True)

    outs = []
    for hh in range(ML_HEADS):
        sl = slice(hh * dh, (hh + 1) * dh)
        b_col = lane_pick(b_cols, ML_HEADS + hh)
        li_col = lane_pick(gc, hh)
        r_row = row_pick(gr, hh) - row_pick(b_rows, ML_HEADS + hh)
        m_prev = m_ref[hh]
        dmat = jnp.where(causal, b_col + r_row, -jnp.inf)
        m_inter = b_col + m_prev
        m_t = jnp.maximum(m_inter, jnp.max(dmat, axis=-1, keepdims=True))
        qh = qs[:, sl]
        kh = ks[:, sl]
        qb = qh.astype(BF16)
        vb = vv[:, sl].astype(BF16)
        w_intra = lax.dot_general(qb, kh.astype(BF16), (((1,), (1,)), ((), ())),
                                  preferred_element_type=F32) * jnp.exp(dmat - m_t)
        inter = jnp.exp(m_inter - m_t)
        c_prev = c_ref[hh]
        n_prev = n_ref[hh]
        num = (inter * jnp.dot(qb, c_prev.astype(BF16), preferred_element_type=F32)
               + jnp.dot(w_intra.astype(BF16), vb, preferred_element_type=F32))
        den = (inter * jnp.sum(qh * n_prev, axis=-1, keepdims=True)
               + jnp.sum(w_intra, axis=-1, keepdims=True))
        hout = num / jnp.maximum(jnp.abs(den), jnp.exp(-m_t))
        outs.append(_rms(hout, ng_ref[0, :, sl]))
        b_last = jnp.sum(jnp.where(t_col == ln - 1, b_col, 0.0), axis=0, keepdims=True)
        g_col = b_last - b_col + li_col
        m_next = jnp.maximum(b_last + m_prev, jnp.max(g_col, axis=0, keepdims=True))
        decay = jnp.exp(b_last + m_prev - m_next)
        kw = kh * jnp.exp(g_col - m_next)
        c_ref[hh] = decay * c_prev + lax.dot_general(kw.astype(BF16), vb, (((0,), (0,)), ((), ())),
                                                     preferred_element_type=F32)
        n_ref[hh] = decay * n_prev + jnp.sum(kw, axis=0, keepdims=True)
        m_ref[hh] = m_next
    return (jnp.concatenate(outs, axis=-1) * og).astype(BF16)


def _mlstm(z, zg, g_rows, cw, cb, brow, bcol, ng, l):
    b, s, _ = z.shape
    w = ML_WIDTH
    c0 = (2 * RG_WIDTH + 3 * DA_WIDTH) // w
    return pl.pallas_call(
        _mlstm_kernel,
        grid=(b, s // T_ML),
        in_specs=[pl.BlockSpec((1, T_ML, w), lambda bi, ci: (bi, ci, c0)),
                  pl.BlockSpec((1, T_ML, w), lambda bi, ci: (bi, ci, c0 + 1)),
                  pl.BlockSpec((1, T_ML, w), lambda bi, ci: (bi, ci, c0 + 2)),
                  pl.BlockSpec((1, T_ML, w), lambda bi, ci: (bi, ci, c0 + 3)),
                  pl.BlockSpec((1, T_ML, LANES), lambda bi, ci: (bi, ci, 0)),
                  pl.BlockSpec((1, SUBLANES, T_ML), lambda bi, ci: (bi, 0, ci)),
                  _layer(cw, l), _layer(cb, l), _full((1, LANES)), _full((SUBLANES, 1)), _layer(ng, l)],
        out_specs=pl.BlockSpec((1, T_ML, w), lambda bi, ci: (bi, ci, 0)),
        out_shape=jax.ShapeDtypeStruct((b, s, w), BF16),
        scratch_shapes=[pltpu.VMEM((SUBLANES, w), F32), pltpu.VMEM((SUBLANES, w), F32),
                        pltpu.VMEM((ML_HEADS, ML_HEAD_DIM, ML_HEAD_DIM), F32),
                        pltpu.VMEM((ML_HEADS, 1, ML_HEAD_DIM), F32),
                        pltpu.VMEM((ML_HEADS, 1, 1), F32)],
        compiler_params=_params(2),
        name="mlstm",
    )(z, z, z, z, zg, g_rows, cw, cb, brow, bcol, ng)


def _outproj_kernel(x_ref, yr_ref, ya_ref, ym_ref, w_ref, g_ref, x1_ref, h_ref):
    acc = x_ref[...]
    acc = acc + jnp.dot(yr_ref[...], w_ref[0, 0:RG_WIDTH, :], preferred_element_type=F32)
    acc = acc + jnp.dot(ya_ref[...], w_ref[0, RG_WIDTH:RG_WIDTH + DA_WIDTH, :], preferred_element_type=F32)
    acc = acc + jnp.dot(ym_ref[...], w_ref[0, RG_WIDTH + DA_WIDTH:, :], preferred_element_type=F32)
    x1_ref[...] = acc
    h_ref[...] = _rms(acc, g_ref[0]).astype(BF16)


def _outproj(x2d, y_rg, y_da, y_ml, w_out, g, l):
    t, d = x2d.shape
    row = lambda wd: pl.BlockSpec((TM_OUT, wd), lambda i: (i, 0))
    return pl.pallas_call(
        _outproj_kernel,
        grid=(t // TM_OUT,),
        in_specs=[row(d), row(RG_WIDTH), row(DA_WIDTH), row(ML_WIDTH), _layer(w_out, l), _layer(g, l)],
        out_specs=[row(d), row(d)],
        out_shape=[jax.ShapeDtypeStruct((t, d), F32), jax.ShapeDtypeStruct((t, d), BF16)],
        compiler_params=_params(1),
        name="outproj",
    )(x2d, y_rg, y_da, y_ml, w_out, g)


def _mlp_kernel(h_ref, x_ref, wup_ref, cw_ref, cb_ref, wdn_ref, ng_ref, o_ref, tail_ref, u_ref, act_ref,
                hs_ref, *, final_norm):
    @pl.when(pl.program_id(1) == 0)
    def _():
        tail_ref[...] = jnp.zeros_like(tail_ref)

    tm = h_ref.shape[1]
    hs_ref[...] = h_ref[0]
    n_ch = D_FF // CH_FF
    assert n_ch % 2 == 1 and n_ch * CH_FF == D_FF

    def cols(half, j):
        return pl.ds(half * D_FF + j * CH_FF, CH_FF)

    def up(j, slot):
        for half in range(2):
            u_ref[slot, half] = jnp.dot(hs_ref[...], wup_ref[0, :, cols(half, j)], preferred_element_type=F32)

    def gate(j, slot):
        out_cols = pl.ds(j * CH_FF, CH_FF)
        taps = [(cw_ref[0, :, cols(half, j)] * sc, cb_ref[0, :, cols(half, j)] * sc) for half, sc in ((0, 0.5), (1, 1.0))]
        for r0 in range(0, tm, RB_FF):
            halves = []
            for half in range(2):
                prev = tail_ref[:, cols(half, j)] if r0 == 0 else u_ref[slot, half, r0 - SUBLANES:r0, :]
                halves.append(_causal_conv(u_ref[slot, half, r0:r0 + RB_FF, :], prev, *taps[half]))
            g_half = halves[0]
            act_ref[r0:r0 + RB_FF, out_cols] = (g_half * (1.0 + jnp.tanh(g_half)) * halves[1]).astype(BF16)
        for half in range(2):
            tail_ref[:, cols(half, j)] = u_ref[slot, half, tm - SUBLANES:, :]

    up(0, 0)
    for j in range(n_ch):
        if j + 1 < n_ch:
            up(j + 1, (j + 1) % 2)
        gate(j, j % 2)

    y = x_ref[0] + jnp.dot(act_ref[...], wdn_ref[0], preferred_element_type=F32)
    if final_norm:
        y = _rms(y, ng_ref[...])
    o_ref[0] = y


def _mlp(h, x1, w_up, cw, cb, w_down, ng, final_norm, l):
    b, s, d = x1.shape
    tile = pl.BlockSpec((1, TM_MLP, d), lambda bi, ti: (bi, ti, 0))
    return pl.pallas_call(
        functools.partial(_mlp_kernel, final_norm=final_norm),
        grid=(b, s // TM_MLP),
        in_specs=[tile, tile, _layer(w_up, l), _layer(cw, l), _layer(cb, l), _layer(w_down, l), _full((1, d))],
        out_specs=tile,
        out_shape=jax.ShapeDtypeStruct((b, s, d), F32),
        scratch_shapes=[pltpu.VMEM((SUBLANES, 2 * D_FF), F32), pltpu.VMEM((2, 2, TM_MLP, CH_FF), F32),
                        pltpu.VMEM((TM_MLP, D_FF), BF16), pltpu.VMEM((TM_MLP, d), BF16)],
        compiler_params=_params(2),
        name="mlp",
    )(h, x1, w_up, cw, cb, w_down, ng)


def kernel(x, positions, attn_norm, w_in, rg_conv_w, rg_conv_b, rg_wa, rg_ba, rg_wx, rg_bx, rg_lambda, rg_norm, da_lambda, da_norm, ml_conv_w, ml_conv_b, ml_i_bias, ml_f_bias, ml_norm, w_out, mlp_norm, w_up, ffn_conv_w, ffn_conv_b, w_down, final_norm):
    b, s, d = x.shape
    depth = w_in.shape[0]
    t = b * s
    n_gate = 2 * ML_HEADS
    cos_t, sin_t = _rope_tables(positions)
    w_in_b, w_out_b, w_up_b, w_down_b = (w.astype(BF16) for w in (w_in, w_out, w_up, w_down))
    wa_b, wx_b = rg_wa.astype(BF16), rg_wx.astype(BF16)
    w_gate_b = jnp.pad(w_in[:, :, D_MAIN:], ((0, 0), (0, 0), (0, LANES - n_gate))).astype(BF16)
    gate_bias = jnp.concatenate([ml_i_bias, ml_f_bias], axis=-1)
    attn_norm3, mlp_norm3, rg_norm3, da_norm3, ml_norm3 = map(_rows3, (attn_norm, mlp_norm, rg_norm, da_norm, ml_norm))
    rg_cb3, rg_ba3, rg_bx3, rg_lam3, ml_cb3, ffn_cb3 = map(_rows3, (rg_conv_b, rg_ba, rg_bx, rg_lambda, ml_conv_b, ffn_conv_b))
    for l in range(depth):
        lambda_init = 0.8 - 0.6 * math.exp(-0.3 * l)
        z, zg = _inproj(x.reshape(t, d), attn_norm3, w_in_b, w_gate_b[l], l)
        z = z.reshape(b, s, D_MAIN)
        zg = zg.reshape(b, s, LANES)
        y_rg = _rglru(z, rg_conv_w, rg_cb3, wa_b, rg_ba3, wx_b, rg_bx3, rg_lam3, rg_norm3, l)
        y_da = _attention(z, cos_t, sin_t, da_lambda, da_norm3, lambda_init, l)
        g_rows = jnp.swapaxes(zg[:, :, :n_gate], 1, 2)
        y_ml = _mlstm(z, zg, g_rows, ml_conv_w, ml_cb3,
                      jnp.pad(gate_bias[l], (0, LANES - n_gate)).reshape(1, LANES),
                      gate_bias[l].reshape(n_gate, 1), ml_norm3, l)
        x1, h2 = _outproj(x.reshape(t, d), y_rg.reshape(t, -1), y_da.reshape(t, -1), y_ml.reshape(t, -1),
                          w_out_b, mlp_norm3, l)
        x = _mlp(h2.reshape(b, s, d), x1.reshape(b, s, d), w_up_b, ffn_conv_w, ffn_cb3, w_down_b,
                 final_norm.reshape(1, d), l == depth - 1, l)
    return x
```

```python
import functools
import math

import jax
import jax.numpy as jnp
from jax import lax
from jax.experimental import pallas as pl
from jax.experimental.pallas import tpu as pltpu

F32 = jnp.float32
BF16 = jnp.bfloat16

D_MODEL = 1024
RG_WIDTH = 512
RG_BLOCKS = 4
RG_BLOCK = RG_WIDTH // RG_BLOCKS
RG_CONV = 4
RG_C = 8.0
DA_HEADS = 4
DA_HEAD_DIM = 64
DA_WIDTH = DA_HEADS * 2 * DA_HEAD_DIM
ROPE_THETA = 500000.0
ROPE_DIM = DA_HEAD_DIM // 4
NEG_INF = -1e30
ML_HEADS = 4
ML_HEAD_DIM = 128
ML_WIDTH = ML_HEADS * ML_HEAD_DIM
ML_CONV = 4
D_MIX = RG_WIDTH + DA_WIDTH + ML_WIDTH
D_FF = 2816
FFN_CONV = 3
EPS = 1e-6
IN_WIDTHS = (RG_WIDTH, RG_WIDTH, DA_WIDTH, DA_WIDTH, DA_WIDTH,
             ML_WIDTH, ML_WIDTH, ML_WIDTH, ML_WIDTH, ML_HEADS, ML_HEADS)
D_IN = sum(IN_WIDTHS)

LANES = 128
SUBLANES = 8
D_MAIN = D_IN - 2 * ML_HEADS
ML_Q0 = 2 * RG_WIDTH + 3 * DA_WIDTH
VMEM_LIMIT = 56 * 1024 * 1024

TM_IN = 512
TN_IN = 1536
TM_OUT = 1024
TT_RG = 256
SCAN_BLOCK = SUBLANES * SUBLANES
TQ = 256
TK = 256
L_ML = 256
B_ML = 1
TM_MLP = 1024
CH_FF = 256
RB_FF = 128


def _params(n_axes):
    return pltpu.CompilerParams(dimension_semantics=("arbitrary",) * n_axes,
                                vmem_limit_bytes=VMEM_LIMIT)


def _full(shape):
    nd = len(shape)
    return pl.BlockSpec(shape, lambda *_: (0,) * nd)


def _layer(arr, l):
    nd = arr.ndim
    return pl.BlockSpec((1,) + arr.shape[1:], lambda *_: (l,) + (0,) * (nd - 1),
                        pipeline_mode=pl.Buffered(1))


def _rows3(arr):
    return arr.reshape(arr.shape[0], 1, arr.shape[1])


def _shift_rows(x, prev8, d):
    rolled = pltpu.roll(x, d, axis=0)
    prolled = pltpu.roll(prev8, d, axis=0)
    row = lax.broadcasted_iota(jnp.int32, prev8.shape, 0)
    top = jnp.where(row < d, prolled, rolled[:SUBLANES])
    return jnp.concatenate([top, rolled[SUBLANES:]], axis=0)


def _causal_conv(x, prev8, w, b):
    k = w.shape[0]
    y = x * w[k - 1:k]
    for d in range(1, k):
        y = y + _shift_rows(x, prev8, d) * w[k - 1 - d:k - d]
    return y + b


def _rms(x, g):
    return x * lax.rsqrt(jnp.mean(x * x, axis=-1, keepdims=True) + EPS) * g


def _sigmoid(x):
    return 0.5 + 0.5 * jnp.tanh(0.5 * x)


def _silu(x):
    h = 0.5 * x
    return h * (1.0 + jnp.tanh(h))


def _inproj_kernel(x_ref, xn_ref, g_ref, w_ref, wg_ref, z_ref, zg_ref, ha_ref, hb_ref):
    half = TM_IN // 2
    g = g_ref[0]

    @pl.when(pl.program_id(0) == 0)
    def _():
        ha_ref[...] = _rms(x_ref[:half], g).astype(BF16)

    def project(h, rows):
        for c0 in range(0, D_MAIN, TN_IN):
            z_ref[rows, c0:c0 + TN_IN] = jnp.dot(h, w_ref[0, :, c0:c0 + TN_IN], preferred_element_type=F32)
        zg_ref[rows, :] = jnp.dot(h, wg_ref[...], preferred_element_type=F32)

    hb_ref[...] = _rms(x_ref[half:], g).astype(BF16)
    project(ha_ref[...], slice(0, half))
    ha_next = _rms(xn_ref[...], g).astype(BF16)
    project(hb_ref[...], slice(half, TM_IN))
    ha_ref[...] = ha_next


def _inproj(x2d, g, w_in, w_gate, l):
    t, d = x2d.shape
    half = TM_IN // 2
    last_half = t // half - 1
    return pl.pallas_call(
        _inproj_kernel,
        grid=(t // TM_IN,),
        in_specs=[pl.BlockSpec((TM_IN, d), lambda i: (i, 0)),
                  pl.BlockSpec((half, d), lambda i: (jnp.minimum(2 * i + 2, last_half), 0)),
                  _layer(g, l), _layer(w_in, l), _full((d, LANES))],
        out_specs=[pl.BlockSpec((TM_IN, D_MAIN), lambda i: (i, 0)),
                   pl.BlockSpec((TM_IN, LANES), lambda i: (i, 0))],
        out_shape=[jax.ShapeDtypeStruct((t, D_MAIN), F32), jax.ShapeDtypeStruct((t, LANES), F32)],
        scratch_shapes=[pltpu.VMEM((half, d), BF16), pltpu.VMEM((half, d), BF16)],
        compiler_params=_params(1),
        name="inproj",
    )(x2d, x2d, g, w_in, w_gate)


def _rglru_tile(x, gate, cw_ref, cb_ref, wa_ref, ba_ref, wx_ref, bx_ref, lam_ref, ng_ref,
                prev_ref, carry_ref, a_ref, b_ref, h_ref):
    tt = x.shape[0]
    n_slabs = x.shape[1] // LANES
    u = _causal_conv(x, prev_ref[...], cw_ref[0], cb_ref[0])
    prev_ref[...] = x[tt - SUBLANES:]
    ub = u.astype(BF16)
    ra, ri = [], []
    for n in range(RG_BLOCKS):
        un = ub[:, n * RG_BLOCK:(n + 1) * RG_BLOCK]
        ra.append(jnp.dot(un, wa_ref[0, n], preferred_element_type=F32))
        ri.append(jnp.dot(un, wx_ref[0, n], preferred_element_type=F32))
    r = _sigmoid(jnp.concatenate(ra, axis=-1) + ba_ref[0])
    i = _sigmoid(jnp.concatenate(ri, axis=-1) + bx_ref[0])
    log_a = r * (RG_C * jax.nn.log_sigmoid(lam_ref[0]))
    a = jnp.exp(log_a)
    gain2 = -jnp.tanh(log_a) * (a * a + 1.0)
    gain = jnp.where(gain2 > 0.0, gain2 * lax.rsqrt(gain2), 0.0)
    bt = gain * (i * u)
    for s in range(n_slabs):
        a_ref[s] = a[:, s * LANES:(s + 1) * LANES]
        b_ref[s] = bt[:, s * LANES:(s + 1) * LANES]

    sub = lax.broadcasted_iota(jnp.int32, (SUBLANES, LANES), 0)
    for s in range(n_slabs):
        lanes = slice(s * LANES, (s + 1) * LANES)
        carry = jnp.broadcast_to(carry_ref[:, lanes], (SUBLANES, LANES))
        for base in range(0, tt, SCAN_BLOCK):
            acc_a, acc_h = [], []
            for k in range(SUBLANES):
                rows = pl.ds(base + k, SUBLANES, stride=SUBLANES)
                ak, bk = a_ref[s, rows, :], b_ref[s, rows, :]
                acc_h.append(bk if k == 0 else ak * acc_h[-1] + bk)
                acc_a.append(ak if k == 0 else ak * acc_a[-1])
            seg_a, seg_h = acc_a[-1], acc_h[-1]
            for d in (1, 2, 4):
                keep = sub >= d
                seg_h = jnp.where(keep, seg_a * pltpu.roll(seg_h, d, axis=0) + seg_h, seg_h)
                seg_a = jnp.where(keep, seg_a * pltpu.roll(seg_a, d, axis=0), seg_a)
            seg_end = seg_a * carry + seg_h
            enter = jnp.where(sub == 0, carry, pltpu.roll(seg_end, 1, axis=0))
            for k in range(SUBLANES):
                h_ref[s, pl.ds(base + k, SUBLANES, stride=SUBLANES), :] = acc_h[k] + acc_a[k] * enter
            carry = jnp.broadcast_to(seg_end[SUBLANES - 1:, :], (SUBLANES, LANES))
        carry_ref[:, lanes] = carry[:1]

    h = jnp.concatenate([h_ref[s] for s in range(n_slabs)], axis=-1)
    y = jax.nn.gelu(gate) * h
    return _rms(y, ng_ref[0]).astype(BF16)


def _rglru_kernel(x_ref, gate_ref, cw_ref, cb_ref, wa_ref, ba_ref, wx_ref, bx_ref, lam_ref, ng_ref,
                  y_ref, prev_ref, carry_ref, a_ref, b_ref, h_ref):
    @pl.when(pl.program_id(1) == 0)
    def _():
        prev_ref[...] = jnp.zeros_like(prev_ref)
        carry_ref[...] = jnp.zeros_like(carry_ref)

    y_ref[0] = _rglru_tile(x_ref[0], gate_ref[0], cw_ref, cb_ref, wa_ref, ba_ref, wx_ref, bx_ref, lam_ref, ng_ref,
                           prev_ref, carry_ref, a_ref, b_ref, h_ref)


def _rglru(z, cw, cb, wa, ba, wx, bx, lam, ng, l):
    b, s, _ = z.shape
    w = RG_WIDTH
    slabs = pltpu.VMEM((w // LANES, TT_RG, LANES), F32)
    return pl.pallas_call(
        _rglru_kernel,
        grid=(b, s // TT_RG),
        in_specs=[pl.BlockSpec((1, TT_RG, w), lambda bi, ti: (bi, ti, 0)),
                  pl.BlockSpec((1, TT_RG, w), lambda bi, ti: (bi, ti, 1)),
                  _layer(cw, l), _layer(cb, l), _layer(wa, l), _layer(ba, l), _layer(wx, l), _layer(bx, l),
                  _layer(lam, l), _layer(ng, l)],
        out_specs=pl.BlockSpec((1, TT_RG, w), lambda bi, ti: (bi, ti, 0)),
        out_shape=jax.ShapeDtypeStruct((b, s, w), BF16),
        scratch_shapes=[pltpu.VMEM((SUBLANES, w), F32), pltpu.VMEM((1, w), F32), slabs, slabs, slabs],
        compiler_params=_params(2),
        name="rglru",
    )(z, z, cw, cb, wa, ba, wx, bx, lam, ng)


def _rope_table_kernel(pos_ref, f_ref, c_ref, s_ref):
    ang = pos_ref[0].astype(F32) * f_ref[...]
    cos, sin = jnp.cos(ang), jnp.sin(ang)
    rest = (DA_HEAD_DIM - ROPE_DIM, ang.shape[1])
    c_map = jnp.concatenate([cos, cos, jnp.ones(rest, F32)], axis=0)
    s_map = jnp.concatenate([-sin, sin, jnp.zeros(rest, F32)], axis=0)
    c_ref[0] = jnp.concatenate([c_map, c_map], axis=0).T
    s_ref[0] = jnp.concatenate([s_map, s_map], axis=0).T


def _rope_tables(positions):
    b, s = positions.shape
    half = ROPE_DIM // 2
    inv_freq = ROPE_THETA ** (-jnp.arange(0, ROPE_DIM, 2, dtype=F32) / ROPE_DIM)
    tab = pl.BlockSpec((1, s, 2 * DA_HEAD_DIM), lambda bi: (bi, 0, 0))
    return pl.pallas_call(
        _rope_table_kernel,
        grid=(b,),
        in_specs=[pl.BlockSpec((1, 1, s), lambda bi: (bi, 0, 0)), _full((half, 1))],
        out_specs=[tab, tab],
        out_shape=[jax.ShapeDtypeStruct((b, s, 2 * DA_HEAD_DIM), F32)] * 2,
        compiler_params=_params(1),
        name="rope_tables",
    )(positions.reshape(b, 1, s), inv_freq.reshape(half, 1))


def _rope(x, c, s):
    half = ROPE_DIM // 2
    lane = lax.broadcasted_iota(jnp.int32, (1, LANES), 1)
    first = (lane % DA_HEAD_DIM) < half
    partner = jnp.where(first, pltpu.roll(x, LANES - half, axis=1), pltpu.roll(x, half, axis=1))
    return x * c + partner * s


def _attn_kernel(lam_ref, q_ref, k_ref, v_ref, cos_ref, sin_ref, ng_ref, o_ref,
                 qb_ref, kz_ref, vb_ref, sc_ref, p_ref, *, lambda_init):
    s_len = k_ref.shape[1]
    nt = s_len // TK
    groups = TK // LANES
    lane = lax.broadcasted_iota(jnp.int32, (1, LANES), 1)
    map0 = lane < DA_HEAD_DIM
    q_scale = DA_HEAD_DIM ** -0.5 * math.log2(math.e)

    for j in range(nt):
        rows = slice(j * TK, (j + 1) * TK)
        cos, sin = cos_ref[0, rows, :], sin_ref[0, rows, :]
        kr = _rope(k_ref[0, rows, :], cos, sin)
        kz_ref[0, rows, :] = jnp.where(map0, kr, 0.0).astype(BF16)
        kz_ref[1, rows, :] = jnp.where(map0, 0.0, kr).astype(BF16)
        vb_ref[rows, :] = jnp.concatenate(
            [v_ref[0, rows, :], jnp.broadcast_to(jnp.where(lane == 0, 1.0, 0.0), (TK, LANES))], axis=-1).astype(BF16)
        qb_ref[rows, :] = (_rope(q_ref[0, rows, :], cos, sin) * q_scale).astype(BF16)

    lp = lam_ref[0]
    lam = (jnp.exp(jnp.sum(lp[0:1] * lp[1:2], axis=-1, keepdims=True))
           - jnp.exp(jnp.sum(lp[2:3] * lp[3:4], axis=-1, keepdims=True)) + lambda_init)
    on_or_below_diag = (lax.broadcasted_iota(jnp.int32, (TQ, TK), 0)
                        >= lax.broadcasted_iota(jnp.int32, (TQ, TK), 1))

    for qi in reversed(range(nt)):
        kv = (qi + 1) * TK
        q = qb_ref[qi * TQ:(qi + 1) * TQ, :]
        for c in range(2):
            sc_ref[c, :, :kv] = lax.dot_general(q, kz_ref[c, :kv, :], (((1,), (1,)), ((), ())),
                                                preferred_element_type=F32)
        for c in range(2):
            m = None
            for j in range(qi + 1):
                cols = slice(j * TK, (j + 1) * TK)
                s = sc_ref[c, :, cols]
                if j == qi:
                    s = jnp.where(on_or_below_diag, s, NEG_INF)
                    sc_ref[c, :, cols] = s
                for g in range(groups):
                    sg = s[:, g * LANES:(g + 1) * LANES]
                    m = sg if m is None else jnp.maximum(m, sg)
            m_b = jnp.broadcast_to(jnp.max(m, axis=-1, keepdims=True), (TQ, LANES))
            m_b = jnp.concatenate([m_b] * groups, axis=-1)
            for j in range(qi + 1):
                cols = slice(j * TK, (j + 1) * TK)
                p_ref[c * TQ:(c + 1) * TQ, cols] = jnp.exp2(sc_ref[c, :, cols] - m_b).astype(BF16)
        acc = jnp.dot(p_ref[:, :kv], vb_ref[:kv, :], preferred_element_type=F32)
        hw = 2 * DA_HEAD_DIM
        o = (acc[:TQ, :hw] / acc[:TQ, hw:hw + 1]) - lam * (acc[TQ:, :hw] / acc[TQ:, hw:hw + 1])
        o_ref[0, qi * TQ:(qi + 1) * TQ, :] = (_rms(o, ng_ref[0]) * (1.0 - lambda_init)).astype(BF16)


def _attention(z, cos_t, sin_t, lam_p, ng, lambda_init, l):
    b, s, _ = z.shape
    hw = 2 * DA_HEAD_DIM
    q0 = (2 * RG_WIDTH) // hw
    k0 = q0 + DA_HEADS
    v0 = k0 + DA_HEADS
    seq = lambda c0: pl.BlockSpec((1, s, hw), lambda bi, hi: (bi, 0, c0 + hi))
    tab = pl.BlockSpec((1, s, hw), lambda bi, hi: (bi, 0, 0))
    return pl.pallas_call(
        functools.partial(_attn_kernel, lambda_init=lambda_init),
        grid=(b, DA_HEADS),
        in_specs=[_layer(lam_p, l), seq(q0), seq(k0), seq(v0), tab, tab, _layer(ng, l)],
        out_specs=pl.BlockSpec((1, s, hw), lambda bi, hi: (bi, 0, hi)),
        out_shape=jax.ShapeDtypeStruct((b, s, DA_WIDTH), BF16),
        scratch_shapes=[pltpu.VMEM((s, hw), BF16), pltpu.VMEM((2, s, hw), BF16), pltpu.VMEM((s, 2 * hw), BF16),
                        pltpu.VMEM((2, TQ, s), F32), pltpu.VMEM((2 * TQ, s), BF16)],
        compiler_params=_params(2),
        name="diffattn",
    )(lam_p, z, z, z, cos_t, sin_t, ng)


def _mlstm_kernel(q_ref, k_ref, v_ref, o_ref, gcol_ref, grow_ref, cw_ref, cb_ref, brow_ref, bcol_ref,
                  ng_ref, y_ref, pq_ref, pk_ref, c_ref, n_ref, m_ref):
    @pl.when(pl.program_id(1) == 0)
    def _():
        pq_ref[...] = jnp.zeros_like(pq_ref)
        pk_ref[...] = jnp.zeros_like(pk_ref)
        c_ref[...] = jnp.zeros_like(c_ref)
        n_ref[...] = jnp.zeros_like(n_ref)
        m_ref[...] = jnp.zeros_like(m_ref)

    for bi in range(q_ref.shape[0]):
        q_raw, k_raw = q_ref[bi], k_ref[bi]
        y_ref[bi] = _mlstm_chunk(q_raw, k_raw, pq_ref[bi], pk_ref[bi], v_ref[bi], o_ref[bi], gcol_ref[bi],
                                 grow_ref[bi], cw_ref, cb_ref, brow_ref, bcol_ref, ng_ref,
                                 c_ref.at[bi], n_ref.at[bi], m_ref.at[bi])
        pq_ref[bi] = q_raw[L_ML - SUBLANES:]
        pk_ref[bi] = k_raw[L_ML - SUBLANES:]


def _mlstm_chunk(q_raw, k_raw, pq, pk, vv, o_pre, gcol, grow, cw_ref, cb_ref, brow_ref, bcol_ref, ng_ref,
                 c_ref, n_ref, m_ref):
    w = ML_WIDTH
    dh = ML_HEAD_DIM
    ln = q_raw.shape[0]
    qs = _silu(_causal_conv(q_raw, pq, cw_ref[0, :, :w], cb_ref[0, :, :w]))
    ks = _silu(_causal_conv(k_raw, pk, cw_ref[0, :, w:], cb_ref[0, :, w:])) * (dh ** -0.5)
    og = _sigmoid(o_pre)

    gc = gcol + brow_ref[...]
    gr = grow + bcol_ref[...]
    r_i = lax.broadcasted_iota(jnp.int32, (ln, ln), 0)
    c_i = lax.broadcasted_iota(jnp.int32, (ln, ln), 1)
    causal = r_i >= c_i
    b_cols = jnp.dot(causal.astype(F32), jax.nn.log_sigmoid(gc),
                     precision=lax.Precision.HIGHEST, preferred_element_type=F32)
    b_rows = jnp.dot(jax.nn.log_sigmoid(gr), (r_i <= c_i).astype(F32),
                     precision=lax.Precision.HIGHEST, preferred_element_type=F32)

    lane = lax.broadcasted_iota(jnp.int32, (1, LANES), 1)
    sub = lax.broadcasted_iota(jnp.int32, (SUBLANES, 1), 0)
    t_col = lax.broadcasted_iota(jnp.int32, (ln, 1), 0)

    def lane_pick(a, j):
        return jnp.sum(jnp.where(lane == j, a, 0.0), axis=-1, keepdims=True)

    def row_pick(a, j):
        return jnp.sum(jnp.where(sub == j, a, 0.0), axis=0, keepdims=True)

    outs = []
    for hh in range(ML_HEADS):
        sl = slice(hh * dh, (hh + 1) * dh)
        b_col = lane_pick(b_cols, ML_HEADS + hh)
        li_col = lane_pick(gc, hh)
        r_row = row_pick(gr, hh) - row_pick(b_rows, ML_HEADS + hh)
        m_prev = m_ref[hh]
        dmat = jnp.where(causal, b_col + r_row, -jnp.inf)
        m_inter = b_col + m_prev
        m_t = jnp.maximum(m_inter, jnp.max(dmat, axis=-1, keepdims=True))
        qh = qs[:, sl]
        kh = ks[:, sl]
        qb = qh.astype(BF16)
        vb = vv[:, sl].astype(BF16)
        w_intra = lax.dot_general(qb, kh.astype(BF16), (((1,), (1,)), ((), ())),
                                  preferred_element_type=F32) * jnp.exp(dmat - m_t)
        inter = jnp.exp(m_inter - m_t)
        c_prev = c_ref[hh]
        n_prev = n_ref[hh]
        num = (inter * jnp.dot(qb, c_prev.astype(BF16), preferred_element_type=F32)
               + jnp.dot(w_intra.astype(BF16), vb, preferred_element_type=F32))
        den = (inter * jnp.sum(qh * n_prev, axis=-1, keepdims=True)
               + jnp.sum(w_intra, axis=-1, keepdims=True))
        hout = num / jnp.maximum(jnp.abs(den), jnp.exp(-m_t))
        outs.append(_rms(hout, ng_ref[0, :, sl]))
        b_last = jnp.sum(jnp.where(t_col == ln - 1, b_col, 0.0), axis=0, keepdims=True)
        g_col = b_last - b_col + li_col
        m_next = jnp.maximum(b_last + m_prev, jnp.max(g_col, axis=0, keepdims=True))
        decay = jnp.exp(b_last + m_prev - m_next)
        kw = kh * jnp.exp(g_col - m_next)
        c_ref[hh] = decay * c_prev + lax.dot_general(kw.astype(BF16), vb, (((0,), (0,)), ((), ())),
                                                     preferred_element_type=F32)
        n_ref[hh] = decay * n_prev + jnp.sum(kw, axis=0, keepdims=True)
        m_ref[hh] = m_next
    return (jnp.concatenate(outs, axis=-1) * og).astype(BF16)


def _mlstm(z, zg, g_rows, cw, cb, brow, bcol, ng, l):
    b, s, _ = z.shape
    w = ML_WIDTH
    nb = B_ML
    c0 = ML_Q0 // w
    col = lambda c: pl.BlockSpec((nb, L_ML, w), lambda bi, ci: (bi, ci, c))
    return pl.pallas_call(
        _mlstm_kernel,
        grid=(b // nb, s // L_ML),
        in_specs=[col(c0), col(c0 + 1), col(c0 + 2), col(c0 + 3),
                  pl.BlockSpec((nb, L_ML, LANES), lambda bi, ci: (bi, ci, 0)),
                  pl.BlockSpec((nb, SUBLANES, L_ML), lambda bi, ci: (bi, 0, ci)),
                  _layer(cw, l), _layer(cb, l), _full((1, LANES)), _full((SUBLANES, 1)), _layer(ng, l)],
        out_specs=col(0),
        out_shape=jax.ShapeDtypeStruct((b, s, w), BF16),
        scratch_shapes=[pltpu.VMEM((nb, SUBLANES, w), F32), pltpu.VMEM((nb, SUBLANES, w), F32),
                        pltpu.VMEM((nb, ML_HEADS, ML_HEAD_DIM, ML_HEAD_DIM), F32),
                        pltpu.VMEM((nb, ML_HEADS, 1, ML_HEAD_DIM), F32),
                        pltpu.VMEM((nb, ML_HEADS, 1, 1), F32)],
        compiler_params=_params(2),
        name="mlstm",
    )(z, z, z, z, zg, g_rows, cw, cb, brow, bcol, ng)


def _outproj_kernel(x_ref, yr_ref, ya_ref, ym_ref, w_ref, g_ref, x1_ref, h_ref):
    acc = x_ref[...]
    acc = acc + jnp.dot(yr_ref[...], w_ref[0, 0:RG_WIDTH, :], preferred_element_type=F32)
    acc = acc + jnp.dot(ya_ref[...], w_ref[0, RG_WIDTH:RG_WIDTH + DA_WIDTH, :], preferred_element_type=F32)
    acc = acc + jnp.dot(ym_ref[...], w_ref[0, RG_WIDTH + DA_WIDTH:, :], preferred_element_type=F32)
    x1_ref[...] = acc
    h_ref[...] = _rms(acc, g_ref[0]).astype(BF16)


def _outproj(x2d, y_rg, y_da, y_ml, w_out, g, l):
    t, d = x2d.shape
    row = lambda wd: pl.BlockSpec((TM_OUT, wd), lambda i: (i, 0))
    return pl.pallas_call(
        _outproj_kernel,
        grid=(t // TM_OUT,),
        in_specs=[row(d), row(RG_WIDTH), row(DA_WIDTH), row(ML_WIDTH), _layer(w_out, l), _layer(g, l)],
        out_specs=[row(d), row(d)],
        out_shape=[jax.ShapeDtypeStruct((t, d), F32), jax.ShapeDtypeStruct((t, d), BF16)],
        compiler_params=_params(1),
        name="outproj",
    )(x2d, y_rg, y_da, y_ml, w_out, g)


def _mlp_kernel(h_ref, x_ref, wup_ref, cw_ref, cb_ref, wdn_ref, ng_ref, o_ref, tail_ref, u_ref, act_ref,
                hs_ref, *, final_norm):
    @pl.when(pl.program_id(1) == 0)
    def _():
        tail_ref[...] = jnp.zeros_like(tail_ref)

    tm = h_ref.shape[1]
    hs_ref[...] = h_ref[0]
    n_ch = D_FF // CH_FF
    assert n_ch % 2 == 1 and n_ch * CH_FF == D_FF

    def cols(half, j):
        return pl.ds(half * D_FF + j * CH_FF, CH_FF)

    def up(j, slot):
        for half in range(2):
            u_ref[slot, half] = jnp.dot(hs_ref[...], wup_ref[0, :, cols(half, j)], preferred_element_type=F32)

    def gate(j, slot):
        out_cols = pl.ds(j * CH_FF, CH_FF)
        taps = [(cw_ref[0, :, cols(half, j)] * sc, cb_ref[0, :, cols(half, j)] * sc) for half, sc in ((0, 0.5), (1, 1.0))]
        for r0 in range(0, tm, RB_FF):
            halves = []
            for half in range(2):
                prev = tail_ref[:, cols(half, j)] if r0 == 0 else u_ref[slot, half, r0 - SUBLANES:r0, :]
                halves.append(_causal_conv(u_ref[slot, half, r0:r0 + RB_FF, :], prev, *taps[half]))
            g_half = halves[0]
            act_ref[r0:r0 + RB_FF, out_cols] = (g_half * (1.0 + jnp.tanh(g_half)) * halves[1]).astype(BF16)
        for half in range(2):
            tail_ref[:, cols(half, j)] = u_ref[slot, half, tm - SUBLANES:, :]

    up(0, 0)
    for j in range(n_ch):
        if j + 1 < n_ch:
            up(j + 1, (j + 1) % 2)
        gate(j, j % 2)

    y = x_ref[0] + jnp.dot(act_ref[...], wdn_ref[0], preferred_element_type=F32)
    if final_norm:
        y = _rms(y, ng_ref[...])
    o_ref[0] = y


def _mlp(h, x1, w_up, cw, cb, w_down, ng, final_norm, l):
    b, s, d = x1.shape
    tile = pl.BlockSpec((1, TM_MLP, d), lambda bi, ti: (bi, ti, 0))
    return pl.pallas_call(
        functools.partial(_mlp_kernel, final_norm=final_norm),
        grid=(b, s // TM_MLP),
        in_specs=[tile, tile, _layer(w_up, l), _layer(cw, l), _layer(cb, l), _layer(w_down, l), _full((1, d))],
        out_specs=tile,
        out_shape=jax.ShapeDtypeStruct((b, s, d), F32),
        scratch_shapes=[pltpu.VMEM((SUBLANES, 2 * D_FF), F32), pltpu.VMEM((2, 2, TM_MLP, CH_FF), F32),
                        pltpu.VMEM((TM_MLP, D_FF), BF16), pltpu.VMEM((TM_MLP, d), BF16)],
        compiler_params=_params(2),
        name="mlp",
    )(h, x1, w_up, cw, cb, w_down, ng)


def kernel(x, positions, attn_norm, w_in, rg_conv_w, rg_conv_b, rg_wa, rg_ba, rg_wx, rg_bx, rg_lambda, rg_norm, da_lambda, da_norm, ml_conv_w, ml_conv_b, ml_i_bias, ml_f_bias, ml_norm, w_out, mlp_norm, w_up, ffn_conv_w, ffn_conv_b, w_down, final_norm):
    b, s, d = x.shape
    depth = w_in.shape[0]
    t = b * s
    n_gate = 2 * ML_HEADS
    cos_t, sin_t = _rope_tables(positions)
    w_in_b, w_out_b, w_up_b, w_down_b = (w.astype(BF16) for w in (w_in, w_out, w_up, w_down))
    wa_b, wx_b = rg_wa.astype(BF16), rg_wx.astype(BF16)
    w_gate_b = jnp.pad(w_in[:, :, D_MAIN:], ((0, 0), (0, 0), (0, LANES - n_gate))).astype(BF16)
    gate_bias = jnp.concatenate([ml_i_bias, ml_f_bias], axis=-1)
    attn_norm3, mlp_norm3, rg_norm3, da_norm3, ml_norm3 = map(_rows3, (attn_norm, mlp_norm, rg_norm, da_norm, ml_norm))
    rg_cb3, rg_ba3, rg_bx3, rg_lam3, ml_cb3, ffn_cb3 = map(_rows3, (rg_conv_b, rg_ba, rg_bx, rg_lambda, ml_conv_b, ffn_conv_b))
    for l in range(depth):
        lambda_init = 0.8 - 0.6 * math.exp(-0.3 * l)
        z, zg = _inproj(x.reshape(t, d), attn_norm3, w_in_b, w_gate_b[l], l)
        z = z.reshape(b, s, D_MAIN)
        zg = zg.reshape(b, s, LANES)
        y_rg = _rglru(z, rg_conv_w, rg_cb3, wa_b, rg_ba3, wx_b, rg_bx3, rg_lam3, rg_norm3, l)
        y_da = _attention(z, cos_t, sin_t, da_lambda, da_norm3, lambda_init, l)
        g_rows = jnp.swapaxes(zg[:, :, :n_gate], 1, 2)
        y_ml = _mlstm(z, zg, g_rows, ml_conv_w, ml_cb3,
                      jnp.pad(gate_bias[l], (0, LANES - n_gate)).reshape(1, LANES),
                      gate_bias[l].reshape(n_gate, 1), ml_norm3, l)
        x1, h2 = _outproj(x.reshape(t, d), y_rg.reshape(t, -1), y_da.reshape(t, -1), y_ml.reshape(t, -1),
                          w_out_b, mlp_norm3, l)
        x = _mlp(h2.reshape(b, s, d), x1.reshape(b, s, d), w_up_b, ffn_conv_w, ffn_cb3, w_down_b,
                 final_norm.reshape(1, d), l == depth - 1, l)
    return x
```

```python
import functools
import math

import jax
import jax.numpy as jnp
from jax import lax
from jax.experimental import pallas as pl
from jax.experimental.pallas import tpu as pltpu

F32 = jnp.float32
BF16 = jnp.bfloat16

D_MODEL = 1024
RG_WIDTH = 512
RG_BLOCKS = 4
RG_BLOCK = RG_WIDTH // RG_BLOCKS
RG_CONV = 4
RG_C = 8.0
DA_HEADS = 4
DA_HEAD_DIM = 64
DA_WIDTH = DA_HEADS * 2 * DA_HEAD_DIM
ROPE_THETA = 500000.0
ROPE_DIM = DA_HEAD_DIM // 4
NEG_INF = -1e30
ML_HEADS = 4
ML_HEAD_DIM = 128
ML_WIDTH = ML_HEADS * ML_HEAD_DIM
ML_CONV = 4
D_MIX = RG_WIDTH + DA_WIDTH + ML_WIDTH
D_FF = 2816
FFN_CONV = 3
EPS = 1e-6
IN_WIDTHS = (RG_WIDTH, RG_WIDTH, DA_WIDTH, DA_WIDTH, DA_WIDTH,
             ML_WIDTH, ML_WIDTH, ML_WIDTH, ML_WIDTH, ML_HEADS, ML_HEADS)
D_IN = sum(IN_WIDTHS)

LANES = 128
SUBLANES = 8
BF16_SUBLANES = 16
D_MAIN = D_IN - 2 * ML_HEADS
ML_Q0 = 2 * RG_WIDTH + 3 * DA_WIDTH
VMEM_LIMIT = 56 * 1024 * 1024

TM_IN = 512
TN_IN = 1536
TM_OUT = 1024
TT_RG = 256
SCAN_BLOCK = SUBLANES * SUBLANES
TQ = 256
TK = 256
L_ML = 256
B_ML = 1
TM_MLP = 1024
CH_FF = 256
RB_FF = 128


def _params(n_axes):
    return pltpu.CompilerParams(dimension_semantics=("arbitrary",) * n_axes,
                                vmem_limit_bytes=VMEM_LIMIT)


def _full(shape):
    nd = len(shape)
    return pl.BlockSpec(shape, lambda *_: (0,) * nd)


def _layer(arr, l):
    nd = arr.ndim
    return pl.BlockSpec((1,) + arr.shape[1:], lambda *_: (l,) + (0,) * (nd - 1),
                        pipeline_mode=pl.Buffered(1))


def _cast_specs(w, l, steps, linear=lambda i: i):
    _, r, c = w.shape
    k = 1
    while steps % k or r % (steps // k) or (r // (steps // k)) % BF16_SUBLANES:
        k *= 2
        assert k <= steps, (r, steps)
    rows = r // (steps // k)
    in_spec = pl.BlockSpec((1, rows, c), lambda *g: (l, linear(*g) // k, 0))
    out_spec = pl.BlockSpec((1, rows, c), lambda *g: (0, linear(*g) // k, 0))
    return in_spec, out_spec, jax.ShapeDtypeStruct((1, r, c), BF16)


def _rows3(arr):
    return arr.reshape(arr.shape[0], 1, arr.shape[1])


def _shift_rows(x, prev8, d):
    rolled = pltpu.roll(x, d, axis=0)
    prolled = pltpu.roll(prev8, d, axis=0)
    row = lax.broadcasted_iota(jnp.int32, prev8.shape, 0)
    top = jnp.where(row < d, prolled, rolled[:SUBLANES])
    return jnp.concatenate([top, rolled[SUBLANES:]], axis=0)


def _causal_conv(x, prev8, w, b):
    k = w.shape[0]
    y = x * w[k - 1:k]
    for d in range(1, k):
        y = y + _shift_rows(x, prev8, d) * w[k - 1 - d:k - d]
    return y + b


def _rms(x, g):
    return x * lax.rsqrt(jnp.mean(x * x, axis=-1, keepdims=True) + EPS) * g


def _sigmoid(x):
    return 0.5 + 0.5 * jnp.tanh(0.5 * x)


def _silu(x):
    h = 0.5 * x
    return h * (1.0 + jnp.tanh(h))


def _inproj_kernel(x_ref, xn_ref, g_ref, w_ref, wg_ref, wu_ref, wd_ref, wo_ref,
                   z_ref, zg_ref, wub_ref, wdb_ref, wob_ref, ha_ref, hb_ref):
    half = TM_IN // 2
    g = g_ref[0]
    wub_ref[...] = wu_ref[...].astype(BF16)
    wdb_ref[...] = wd_ref[...].astype(BF16)
    wob_ref[...] = wo_ref[...].astype(BF16)

    @pl.when(pl.program_id(0) == 0)
    def _():
        ha_ref[...] = _rms(x_ref[:half], g).astype(BF16)

    def project(h, rows):
        for c0 in range(0, D_MAIN, TN_IN):
            z_ref[rows, c0:c0 + TN_IN] = jnp.dot(h, w_ref[0, :, c0:c0 + TN_IN], preferred_element_type=F32)
        zg_ref[rows, :] = jnp.dot(h, wg_ref[...], preferred_element_type=F32)

    hb_ref[...] = _rms(x_ref[half:], g).astype(BF16)
    project(ha_ref[...], slice(0, half))
    ha_next = _rms(xn_ref[...], g).astype(BF16)
    project(hb_ref[...], slice(half, TM_IN))
    ha_ref[...] = ha_next


def _inproj(x2d, g, w_in_b, w_gate, w_up, w_down, w_out, l):
    t, d = x2d.shape
    half = TM_IN // 2
    last_half = t // half - 1
    steps = t // TM_IN
    casts = [_cast_specs(w, l, steps) for w in (w_up, w_down, w_out)]
    return pl.pallas_call(
        _inproj_kernel,
        grid=(steps,),
        in_specs=[pl.BlockSpec((TM_IN, d), lambda i: (i, 0)),
                  pl.BlockSpec((half, d), lambda i: (jnp.minimum(2 * i + 2, last_half), 0)),
                  _layer(g, l), _layer(w_in_b, 0), _full((d, LANES))] + [c[0] for c in casts],
        out_specs=[pl.BlockSpec((TM_IN, D_MAIN), lambda i: (i, 0)),
                   pl.BlockSpec((TM_IN, LANES), lambda i: (i, 0))] + [c[1] for c in casts],
        out_shape=[jax.ShapeDtypeStruct((t, D_MAIN), F32), jax.ShapeDtypeStruct((t, LANES), F32)]
                  + [c[2] for c in casts],
        scratch_shapes=[pltpu.VMEM((half, d), BF16), pltpu.VMEM((half, d), BF16)],
        compiler_params=_params(1),
        name="inproj",
    )(x2d, x2d, g, w_in_b, w_gate, w_up, w_down, w_out)


def _rglru_tile(x, gate, cw_ref, cb_ref, wa_ref, ba_ref, wx_ref, bx_ref, lam_ref, ng_ref,
                prev_ref, carry_ref, a_ref, b_ref, h_ref):
    tt = x.shape[0]
    n_slabs = x.shape[1] // LANES
    u = _causal_conv(x, prev_ref[...], cw_ref[0], cb_ref[0])
    prev_ref[...] = x[tt - SUBLANES:]
    ub = u.astype(BF16)
    ra, ri = [], []
    for n in range(RG_BLOCKS):
        un = ub[:, n * RG_BLOCK:(n + 1) * RG_BLOCK]
        ra.append(jnp.dot(un, wa_ref[0, n], preferred_element_type=F32))
        ri.append(jnp.dot(un, wx_ref[0, n], preferred_element_type=F32))
    r = _sigmoid(jnp.concatenate(ra, axis=-1) + ba_ref[0])
    i = _sigmoid(jnp.concatenate(ri, axis=-1) + bx_ref[0])
    log_a = r * (RG_C * jax.nn.log_sigmoid(lam_ref[0]))
    a = jnp.exp(log_a)
    gain2 = -jnp.tanh(log_a) * (a * a + 1.0)
    gain = jnp.where(gain2 > 0.0, gain2 * lax.rsqrt(gain2), 0.0)
    bt = gain * (i * u)
    for s in range(n_slabs):
        a_ref[s] = a[:, s * LANES:(s + 1) * LANES]
        b_ref[s] = bt[:, s * LANES:(s + 1) * LANES]

    sub = lax.broadcasted_iota(jnp.int32, (SUBLANES, LANES), 0)
    for s in range(n_slabs):
        lanes = slice(s * LANES, (s + 1) * LANES)
        carry = jnp.broadcast_to(carry_ref[:, lanes], (SUBLANES, LANES))
        for base in range(0, tt, SCAN_BLOCK):
            acc_a, acc_h = [], []
            for k in range(SUBLANES):
                rows = pl.ds(base + k, SUBLANES, stride=SUBLANES)
                ak, bk = a_ref[s, rows, :], b_ref[s, rows, :]
                acc_h.append(bk if k == 0 else ak * acc_h[-1] + bk)
                acc_a.append(ak if k == 0 else ak * acc_a[-1])
            seg_a, seg_h = acc_a[-1], acc_h[-1]
            for d in (1, 2, 4):
                keep = sub >= d
                seg_h = jnp.where(keep, seg_a * pltpu.roll(seg_h, d, axis=0) + seg_h, seg_h)
                seg_a = jnp.where(keep, seg_a * pltpu.roll(seg_a, d, axis=0), seg_a)
            seg_end = seg_a * carry + seg_h
            enter = jnp.where(sub == 0, carry, pltpu.roll(seg_end, 1, axis=0))
            for k in range(SUBLANES):
                h_ref[s, pl.ds(base + k, SUBLANES, stride=SUBLANES), :] = acc_h[k] + acc_a[k] * enter
            carry = jnp.broadcast_to(seg_end[SUBLANES - 1:, :], (SUBLANES, LANES))
        carry_ref[:, lanes] = carry[:1]

    h = jnp.concatenate([h_ref[s] for s in range(n_slabs)], axis=-1)
    y = jax.nn.gelu(gate) * h
    return _rms(y, ng_ref[0]).astype(BF16)


def _rglru_kernel(x_ref, gate_ref, cw_ref, cb_ref, wa_ref, ba_ref, wx_ref, bx_ref, lam_ref, ng_ref,
                  y_ref, prev_ref, carry_ref, a_ref, b_ref, h_ref):
    @pl.when(pl.program_id(1) == 0)
    def _():
        prev_ref[...] = jnp.zeros_like(prev_ref)
        carry_ref[...] = jnp.zeros_like(carry_ref)

    y_ref[0] = _rglru_tile(x_ref[0], gate_ref[0], cw_ref, cb_ref, wa_ref, ba_ref, wx_ref, bx_ref, lam_ref, ng_ref,
                           prev_ref, carry_ref, a_ref, b_ref, h_ref)


def _rglru(z, cw, cb, wa, ba, wx, bx, lam, ng, l):
    b, s, _ = z.shape
    w = RG_WIDTH
    slabs = pltpu.VMEM((w // LANES, TT_RG, LANES), F32)
    return pl.pallas_call(
        _rglru_kernel,
        grid=(b, s // TT_RG),
        in_specs=[pl.BlockSpec((1, TT_RG, w), lambda bi, ti: (bi, ti, 0)),
                  pl.BlockSpec((1, TT_RG, w), lambda bi, ti: (bi, ti, 1)),
                  _layer(cw, l), _layer(cb, l), _layer(wa, l), _layer(ba, l), _layer(wx, l), _layer(bx, l),
                  _layer(lam, l), _layer(ng, l)],
        out_specs=pl.BlockSpec((1, TT_RG, w), lambda bi, ti: (bi, ti, 0)),
        out_shape=jax.ShapeDtypeStruct((b, s, w), BF16),
        scratch_shapes=[pltpu.VMEM((SUBLANES, w), F32), pltpu.VMEM((1, w), F32), slabs, slabs, slabs],
        compiler_params=_params(2),
        name="rglru",
    )(z, z, cw, cb, wa, ba, wx, bx, lam, ng)


def _rope_table_kernel(pos_ref, f_ref, w_ref, c_ref, s_ref, wb_ref):
    ang = pos_ref[0].astype(F32) * f_ref[...]
    cos, sin = jnp.cos(ang), jnp.sin(ang)
    rest = (DA_HEAD_DIM - ROPE_DIM, ang.shape[1])
    c_map = jnp.concatenate([cos, cos, jnp.ones(rest, F32)], axis=0)
    s_map = jnp.concatenate([-sin, sin, jnp.zeros(rest, F32)], axis=0)
    c_ref[0] = jnp.concatenate([c_map, c_map], axis=0).T
    s_ref[0] = jnp.concatenate([s_map, s_map], axis=0).T
    wb_ref[...] = w_ref[...].astype(BF16)


def _rope_tables(positions, w_in):
    b, s = positions.shape
    half = ROPE_DIM // 2
    inv_freq = ROPE_THETA ** (-jnp.arange(0, ROPE_DIM, 2, dtype=F32) / ROPE_DIM)
    tab = pl.BlockSpec((1, s, 2 * DA_HEAD_DIM), lambda bi: (bi, 0, 0))
    w_in_spec, w_out_spec, w_shape = _cast_specs(w_in, 0, b)
    return pl.pallas_call(
        _rope_table_kernel,
        grid=(b,),
        in_specs=[pl.BlockSpec((1, 1, s), lambda bi: (bi, 0, 0)), _full((half, 1)), w_in_spec],
        out_specs=[tab, tab, w_out_spec],
        out_shape=[jax.ShapeDtypeStruct((b, s, 2 * DA_HEAD_DIM), F32)] * 2 + [w_shape],
        compiler_params=_params(1),
        name="rope_tables",
    )(positions.reshape(b, 1, s), inv_freq.reshape(half, 1), w_in)


def _rope(x, c, s):
    half = ROPE_DIM // 2
    lane = lax.broadcasted_iota(jnp.int32, (1, LANES), 1)
    first = (lane % DA_HEAD_DIM) < half
    partner = jnp.where(first, pltpu.roll(x, LANES - half, axis=1), pltpu.roll(x, half, axis=1))
    return x * c + partner * s


def _attn_kernel(lam_ref, q_ref, k_ref, v_ref, cos_ref, sin_ref, ng_ref, o_ref,
                 qb_ref, kz_ref, vb_ref, sc_ref, p_ref, *, lambda_init):
    s_len = k_ref.shape[1]
    nt = s_len // TK
    groups = TK // LANES
    lane = lax.broadcasted_iota(jnp.int32, (1, LANES), 1)
    map0 = lane < DA_HEAD_DIM
    q_scale = DA_HEAD_DIM ** -0.5 * math.log2(math.e)

    for j in range(nt):
        rows = slice(j * TK, (j + 1) * TK)
        cos, sin = cos_ref[0, rows, :], sin_ref[0, rows, :]
        kr = _rope(k_ref[0, rows, :], cos, sin)
        kz_ref[0, rows, :] = jnp.where(map0, kr, 0.0).astype(BF16)
        kz_ref[1, rows, :] = jnp.where(map0, 0.0, kr).astype(BF16)
        vb_ref[rows, :] = jnp.concatenate(
            [v_ref[0, rows, :], jnp.broadcast_to(jnp.where(lane == 0, 1.0, 0.0), (TK, LANES))], axis=-1).astype(BF16)
        qb_ref[rows, :] = (_rope(q_ref[0, rows, :], cos, sin) * q_scale).astype(BF16)

    lp = lam_ref[0]
    lam = (jnp.exp(jnp.sum(lp[0:1] * lp[1:2], axis=-1, keepdims=True))
           - jnp.exp(jnp.sum(lp[2:3] * lp[3:4], axis=-1, keepdims=True)) + lambda_init)
    on_or_below_diag = (lax.broadcasted_iota(jnp.int32, (TQ, TK), 0)
                        >= lax.broadcasted_iota(jnp.int32, (TQ, TK), 1))

    for qi in reversed(range(nt)):
        kv = (qi + 1) * TK
        q = qb_ref[qi * TQ:(qi + 1) * TQ, :]
        for c in range(2):
            sc_ref[c, :, :kv] = lax.dot_general(q, kz_ref[c, :kv, :], (((1,), (1,)), ((), ())),
                                                preferred_element_type=F32)
        for c in range(2):
            m = None
            for j in range(qi + 1):
                cols = slice(j * TK, (j + 1) * TK)
                s = sc_ref[c, :, cols]
                if j == qi:
                    s = jnp.where(on_or_below_diag, s, NEG_INF)
                    sc_ref[c, :, cols] = s
                for g in range(groups):
                    sg = s[:, g * LANES:(g + 1) * LANES]
                    m = sg if m is None else jnp.maximum(m, sg)
            m_b = jnp.broadcast_to(jnp.max(m, axis=-1, keepdims=True), (TQ, LANES))
            m_b = jnp.concatenate([m_b] * groups, axis=-1)
            for j in range(qi + 1):
                cols = slice(j * TK, (j + 1) * TK)
                p_ref[c * TQ:(c + 1) * TQ, cols] = jnp.exp2(sc_ref[c, :, cols] - m_b).astype(BF16)
        acc = jnp.dot(p_ref[:, :kv], vb_ref[:kv, :], preferred_element_type=F32)
        hw = 2 * DA_HEAD_DIM
        o = (acc[:TQ, :hw] / acc[:TQ, hw:hw + 1]) - lam * (acc[TQ:, :hw] / acc[TQ:, hw:hw + 1])
        o_ref[0, qi * TQ:(qi + 1) * TQ, :] = (_rms(o, ng_ref[0]) * (1.0 - lambda_init)).astype(BF16)


def _attention(z, cos_t, sin_t, lam_p, ng, lambda_init, l):
    b, s, _ = z.shape
    hw = 2 * DA_HEAD_DIM
    q0 = (2 * RG_WIDTH) // hw
    k0 = q0 + DA_HEADS
    v0 = k0 + DA_HEADS
    seq = lambda c0: pl.BlockSpec((1, s, hw), lambda bi, hi: (bi, 0, c0 + hi))
    tab = pl.BlockSpec((1, s, hw), lambda bi, hi: (bi, 0, 0))
    return pl.pallas_call(
        functools.partial(_attn_kernel, lambda_init=lambda_init),
        grid=(b, DA_HEADS),
        in_specs=[_layer(lam_p, l), seq(q0), seq(k0), seq(v0), tab, tab, _layer(ng, l)],
        out_specs=pl.BlockSpec((1, s, hw), lambda bi, hi: (bi, 0, hi)),
        out_shape=jax.ShapeDtypeStruct((b, s, DA_WIDTH), BF16),
        scratch_shapes=[pltpu.VMEM((s, hw), BF16), pltpu.VMEM((2, s, hw), BF16), pltpu.VMEM((s, 2 * hw), BF16),
                        pltpu.VMEM((2, TQ, s), F32), pltpu.VMEM((2 * TQ, s), BF16)],
        compiler_params=_params(2),
        name="diffattn",
    )(lam_p, z, z, z, cos_t, sin_t, ng)


def _mlstm_kernel(q_ref, k_ref, v_ref, o_ref, gcol_ref, grow_ref, cw_ref, cb_ref, brow_ref, bcol_ref,
                  ng_ref, y_ref, pq_ref, pk_ref, c_ref, n_ref, m_ref):
    @pl.when(pl.program_id(1) == 0)
    def _():
        pq_ref[...] = jnp.zeros_like(pq_ref)
        pk_ref[...] = jnp.zeros_like(pk_ref)
        c_ref[...] = jnp.zeros_like(c_ref)
        n_ref[...] = jnp.zeros_like(n_ref)
        m_ref[...] = jnp.zeros_like(m_ref)

    for bi in range(q_ref.shape[0]):
        q_raw, k_raw = q_ref[bi], k_ref[bi]
        y_ref[bi] = _mlstm_chunk(q_raw, k_raw, pq_ref[bi], pk_ref[bi], v_ref[bi], o_ref[bi], gcol_ref[bi],
                                 grow_ref[bi], cw_ref, cb_ref, brow_ref, bcol_ref, ng_ref,
                                 c_ref.at[bi], n_ref.at[bi], m_ref.at[bi])
        pq_ref[bi] = q_raw[L_ML - SUBLANES:]
        pk_ref[bi] = k_raw[L_ML - SUBLANES:]


def _mlstm_chunk(q_raw, k_raw, pq, pk, vv, o_pre, gcol, grow, cw_ref, cb_ref, brow_ref, bcol_ref, ng_ref,
                 c_ref, n_ref, m_ref):
    w = ML_WIDTH
    dh = ML_HEAD_DIM
    ln = q_raw.shape[0]
    qs = _silu(_causal_conv(q_raw, pq, cw_ref[0, :, :w], cb_ref[0, :, :w]))
    ks = _silu(_causal_conv(k_raw, pk, cw_ref[0, :, w:], cb_ref[0, :, w:])) * (dh ** -0.5)
    og = _sigmoid(o_pre)

    gc = gcol + brow_ref[...]
    gr = grow + bcol_ref[...]
    r_i = lax.broadcasted_iota(jnp.int32, (ln, ln), 0)
    c_i = lax.broadcasted_iota(jnp.int32, (ln, ln), 1)
    causal = r_i >= c_i
    b_cols = jnp.dot(causal.astype(F32), jax.nn.log_sigmoid(gc),
                     precision=lax.Precision.HIGHEST, preferred_element_type=F32)
    b_rows = jnp.dot(jax.nn.log_sigmoid(gr), (r_i <= c_i).astype(F32),
                     precision=lax.Precision.HIGHEST, preferred_element_type=F32)

    lane = lax.broadcasted_iota(jnp.int32, (1, LANES), 1)
    sub = lax.broadcasted_iota(jnp.int32, (SUBLANES, 1), 0)
    t_col = lax.broadcasted_iota(jnp.int32, (ln, 1), 0)

    def lane_pick(a, j):
        return jnp.sum(jnp.where(lane == j, a, 0.0), axis=-1, keepdims=True)

    def row_pick(a, j):
        return jnp.sum(jnp.where(sub == j, a, 0.0), axis=0, keepdims=True)

    outs = []
    for hh in range(ML_HEADS):
        sl = slice(hh * dh, (hh + 1) * dh)
        b_col = lane_pick(b_cols, ML_HEADS + hh)
        li_col = lane_pick(gc, hh)
        r_row = row_pick(gr, hh) - row_pick(b_rows, ML_HEADS + hh)
        m_prev = m_ref[hh]
        dmat = jnp.where(causal, b_col + r_row, -jnp.inf)
        m_inter = b_col + m_prev
        m_t = jnp.maximum(m_inter, jnp.max(dmat, axis=-1, keepdims=True))
        qh = qs[:, sl]
        kh = ks[:, sl]
        qb = qh.astype(BF16)
        vb = vv[:, sl].astype(BF16)
        w_intra = lax.dot_general(qb, kh.astype(BF16), (((1,), (1,)), ((), ())),
                                  preferred_element_type=F32) * jnp.exp(dmat - m_t)
        inter = jnp.exp(m_inter - m_t)
        c_prev = c_ref[hh]
        n_prev = n_ref[hh]
        num = (inter * jnp.dot(qb, c_prev.astype(BF16), preferred_element_type=F32)
               + jnp.dot(w_intra.astype(BF16), vb, preferred_element_type=F32))
        den = (inter * jnp.sum(qh * n_prev, axis=-1, keepdims=True)
               + jnp.sum(w_intra, axis=-1, keepdims=True))
        hout = num / jnp.maximum(jnp.abs(den), jnp.exp(-m_t))
        outs.append(_rms(hout, ng_ref[0, :, sl]))
        b_last = jnp.sum(jnp.where(t_col == ln - 1, b_col, 0.0), axis=0, keepdims=True)
        g_col = b_last - b_col + li_col
        m_next = jnp.maximum(b_last + m_prev, jnp.max(g_col, axis=0, keepdims=True))
        decay = jnp.exp(b_last + m_prev - m_next)
        kw = kh * jnp.exp(g_col - m_next)
        c_ref[hh] = decay * c_prev + lax.dot_general(kw.astype(BF16), vb, (((0,), (0,)), ((), ())),
                                                     preferred_element_type=F32)
        n_ref[hh] = decay * n_prev + jnp.sum(kw, axis=0, keepdims=True)
        m_ref[hh] = m_next
    return (jnp.concatenate(outs, axis=-1) * og).astype(BF16)


def _mlstm(z, zg, g_rows, cw, cb, brow, bcol, ng, l):
    b, s, _ = z.shape
    w = ML_WIDTH
    nb = B_ML
    c0 = ML_Q0 // w
    col = lambda c: pl.BlockSpec((nb, L_ML, w), lambda bi, ci: (bi, ci, c))
    return pl.pallas_call(
        _mlstm_kernel,
        grid=(b // nb, s // L_ML),
        in_specs=[col(c0), col(c0 + 1), col(c0 + 2), col(c0 + 3),
                  pl.BlockSpec((nb, L_ML, LANES), lambda bi, ci: (bi, ci, 0)),
                  pl.BlockSpec((nb, SUBLANES, L_ML), lambda bi, ci: (bi, 0, ci)),
                  _layer(cw, l), _layer(cb, l), _full((1, LANES)), _full((SUBLANES, 1)), _layer(ng, l)],
        out_specs=col(0),
        out_shape=jax.ShapeDtypeStruct((b, s, w), BF16),
        scratch_shapes=[pltpu.VMEM((nb, SUBLANES, w), F32), pltpu.VMEM((nb, SUBLANES, w), F32),
                        pltpu.VMEM((nb, ML_HEADS, ML_HEAD_DIM, ML_HEAD_DIM), F32),
                        pltpu.VMEM((nb, ML_HEADS, 1, ML_HEAD_DIM), F32),
                        pltpu.VMEM((nb, ML_HEADS, 1, 1), F32)],
        compiler_params=_params(2),
        name="mlstm",
    )(z, z, z, z, zg, g_rows, cw, cb, brow, bcol, ng)


def _outproj_kernel(x_ref, yr_ref, ya_ref, ym_ref, w_ref, g_ref, x1_ref, h_ref):
    acc = x_ref[...]
    acc = acc + jnp.dot(yr_ref[...], w_ref[0, 0:RG_WIDTH, :], preferred_element_type=F32)
    acc = acc + jnp.dot(ya_ref[...], w_ref[0, RG_WIDTH:RG_WIDTH + DA_WIDTH, :], preferred_element_type=F32)
    acc = acc + jnp.dot(ym_ref[...], w_ref[0, RG_WIDTH + DA_WIDTH:, :], preferred_element_type=F32)
    x1_ref[...] = acc
    h_ref[...] = _rms(acc, g_ref[0]).astype(BF16)


def _outproj(x2d, y_rg, y_da, y_ml, w_out, g, l):
    t, d = x2d.shape
    row = lambda wd: pl.BlockSpec((TM_OUT, wd), lambda i: (i, 0))
    return pl.pallas_call(
        _outproj_kernel,
        grid=(t // TM_OUT,),
        in_specs=[row(d), row(RG_WIDTH), row(DA_WIDTH), row(ML_WIDTH), _layer(w_out, 0), _layer(g, l)],
        out_specs=[row(d), row(d)],
        out_shape=[jax.ShapeDtypeStruct((t, d), F32), jax.ShapeDtypeStruct((t, d), BF16)],
        compiler_params=_params(1),
        name="outproj",
    )(x2d, y_rg, y_da, y_ml, w_out, g)


def _mlp_kernel(*refs, final_norm):
    if final_norm:
        h_ref, x_ref, wup_ref, cw_ref, cb_ref, wdn_ref, ng_ref, o_ref, tail_ref, u_ref, act_ref, hs_ref = refs
    else:
        (h_ref, x_ref, wup_ref, cw_ref, cb_ref, wdn_ref, ng_ref, wn_ref,
         o_ref, wnb_ref, tail_ref, u_ref, act_ref, hs_ref) = refs
        wnb_ref[...] = wn_ref[...].astype(BF16)

    @pl.when(pl.program_id(1) == 0)
    def _():
        tail_ref[...] = jnp.zeros_like(tail_ref)

    tm = h_ref.shape[1]
    hs_ref[...] = h_ref[0]
    n_ch = D_FF // CH_FF
    assert n_ch % 2 == 1 and n_ch * CH_FF == D_FF

    def cols(half, j):
        return pl.ds(half * D_FF + j * CH_FF, CH_FF)

    def up(j, slot):
        for half in range(2):
            u_ref[slot, half] = jnp.dot(hs_ref[...], wup_ref[0, :, cols(half, j)], preferred_element_type=F32)

    def gate(j, slot):
        out_cols = pl.ds(j * CH_FF, CH_FF)
        taps = [(cw_ref[0, :, cols(half, j)] * sc, cb_ref[0, :, cols(half, j)] * sc) for half, sc in ((0, 0.5), (1, 1.0))]
        for r0 in range(0, tm, RB_FF):
            halves = []
            for half in range(2):
                prev = tail_ref[:, cols(half, j)] if r0 == 0 else u_ref[slot, half, r0 - SUBLANES:r0, :]
                halves.append(_causal_conv(u_ref[slot, half, r0:r0 + RB_FF, :], prev, *taps[half]))
            g_half = halves[0]
            act_ref[r0:r0 + RB_FF, out_cols] = (g_half * (1.0 + jnp.tanh(g_half)) * halves[1]).astype(BF16)
        for half in range(2):
            tail_ref[:, cols(half, j)] = u_ref[slot, half, tm - SUBLANES:, :]

    up(0, 0)
    for j in range(n_ch):
        if j + 1 < n_ch:
            up(j + 1, (j + 1) % 2)
        gate(j, j % 2)

    y = x_ref[0] + jnp.dot(act_ref[...], wdn_ref[0], preferred_element_type=F32)
    if final_norm:
        y = _rms(y, ng_ref[...])
    o_ref[0] = y


def _mlp(h, x1, w_up_b, cw, cb, w_down_b, ng, w_in, l):
    b, s, d = x1.shape
    nt = s // TM_MLP
    final_norm = l == w_in.shape[0] - 1
    tile = pl.BlockSpec((1, TM_MLP, d), lambda bi, ti: (bi, ti, 0))
    in_specs = [tile, tile, _layer(w_up_b, 0), _layer(cw, l), _layer(cb, l), _layer(w_down_b, 0), _full((1, d))]
    out_specs = [tile]
    out_shape = [jax.ShapeDtypeStruct((b, s, d), F32)]
    args = [h, x1, w_up_b, cw, cb, w_down_b, ng]
    if not final_norm:
        w_spec, wb_spec, wb_shape = _cast_specs(w_in, l + 1, b * nt, lambda bi, ti: bi * nt + ti)
        in_specs.append(w_spec)
        out_specs.append(wb_spec)
        out_shape.append(wb_shape)
        args.append(w_in)
    outs = pl.pallas_call(
        functools.partial(_mlp_kernel, final_norm=final_norm),
        grid=(b, nt),
        in_specs=in_specs,
        out_specs=out_specs,
        out_shape=out_shape,
        scratch_shapes=[pltpu.VMEM((SUBLANES, 2 * D_FF), F32), pltpu.VMEM((2, 2, TM_MLP, CH_FF), F32),
                        pltpu.VMEM((TM_MLP, D_FF), BF16), pltpu.VMEM((TM_MLP, d), BF16)],
        compiler_params=_params(2),
        name="mlp",
    )(*args)
    return outs[0], (None if final_norm else outs[1])


def kernel(x, positions, attn_norm, w_in, rg_conv_w, rg_conv_b, rg_wa, rg_ba, rg_wx, rg_bx, rg_lambda, rg_norm, da_lambda, da_norm, ml_conv_w, ml_conv_b, ml_i_bias, ml_f_bias, ml_norm, w_out, mlp_norm, w_up, ffn_conv_w, ffn_conv_b, w_down, final_norm):
    b, s, d = x.shape
    depth = w_in.shape[0]
    t = b * s
    n_gate = 2 * ML_HEADS
    cos_t, sin_t, w_in_b = _rope_tables(positions, w_in)
    wa_b, wx_b = rg_wa.astype(BF16), rg_wx.astype(BF16)
    w_gate_b = jnp.pad(w_in[:, :, D_MAIN:], ((0, 0), (0, 0), (0, LANES - n_gate))).astype(BF16)
    gate_bias = jnp.concatenate([ml_i_bias, ml_f_bias], axis=-1)
    attn_norm3, mlp_norm3, rg_norm3, da_norm3, ml_norm3 = map(_rows3, (attn_norm, mlp_norm, rg_norm, da_norm, ml_norm))
    rg_cb3, rg_ba3, rg_bx3, rg_lam3, ml_cb3, ffn_cb3 = map(_rows3, (rg_conv_b, rg_ba, rg_bx, rg_lambda, ml_conv_b, ffn_conv_b))
    for l in range(depth):
        lambda_init = 0.8 - 0.6 * math.exp(-0.3 * l)
        z, zg, w_up_b, w_down_b, w_out_b = _inproj(x.reshape(t, d), attn_norm3, w_in_b, w_gate_b[l],
                                                   w_up, w_down, w_out, l)
        z = z.reshape(b, s, D_MAIN)
        zg = zg.reshape(b, s, LANES)
        y_rg = _rglru(z, rg_conv_w, rg_cb3, wa_b, rg_ba3, wx_b, rg_bx3, rg_lam3, rg_norm3, l)
        y_da = _attention(z, cos_t, sin_t, da_lambda, da_norm3, lambda_init, l)
        g_rows = jnp.swapaxes(zg[:, :, :n_gate], 1, 2)
        y_ml = _mlstm(z, zg, g_rows, ml_conv_w, ml_cb3,
                      jnp.pad(gate_bias[l], (0, LANES - n_gate)).reshape(1, LANES),
                      gate_bias[l].reshape(n_gate, 1), ml_norm3, l)
        x1, h2 = _outproj(x.reshape(t, d), y_rg.reshape(t, -1), y_da.reshape(t, -1), y_ml.reshape(t, -1),
                          w_out_b, mlp_norm3, l)
        x, w_in_b = _mlp(h2.reshape(b, s, d), x1.reshape(b, s, d), w_up_b, ffn_conv_w, ffn_cb3, w_down_b,
                         final_norm.reshape(1, d), w_in, l)
    return x
```

```python
import functools
import math

import jax
import jax.numpy as jnp
from jax import lax
from jax.experimental import pallas as pl
from jax.experimental.pallas import tpu as pltpu

F32 = jnp.float32
BF16 = jnp.bfloat16

D_MODEL = 1024
RG_WIDTH = 512
RG_BLOCKS = 4
RG_BLOCK = RG_WIDTH // RG_BLOCKS
RG_CONV = 4
RG_C = 8.0
DA_HEADS = 4
DA_HEAD_DIM = 64
DA_WIDTH = DA_HEADS * 2 * DA_HEAD_DIM
ROPE_THETA = 500000.0
ROPE_DIM = DA_HEAD_DIM // 4
NEG_INF = -1e30
ML_HEADS = 4
ML_HEAD_DIM = 128
ML_WIDTH = ML_HEADS * ML_HEAD_DIM
ML_CONV = 4
D_MIX = RG_WIDTH + DA_WIDTH + ML_WIDTH
D_FF = 2816
FFN_CONV = 3
EPS = 1e-6
IN_WIDTHS = (RG_WIDTH, RG_WIDTH, DA_WIDTH, DA_WIDTH, DA_WIDTH,
             ML_WIDTH, ML_WIDTH, ML_WIDTH, ML_WIDTH, ML_HEADS, ML_HEADS)
D_IN = sum(IN_WIDTHS)

LANES = 128
SUBLANES = 8
BF16_SUBLANES = 16
D_MAIN = D_IN - 2 * ML_HEADS
ML_Q0 = 2 * RG_WIDTH + 3 * DA_WIDTH
VMEM_LIMIT = 56 * 1024 * 1024

TM_IN = 512
TN_IN = 1536
TM_OUT = 1024
TT_RG = 256
SCAN_BLOCK = SUBLANES * SUBLANES
TQ = 256
TK = 256
L_ML = 256
TM_MLP = 1024
CH_FF = 256
RB_FF = 128


def _params(n_axes):
    return pltpu.CompilerParams(dimension_semantics=("arbitrary",) * n_axes,
                                vmem_limit_bytes=VMEM_LIMIT)


def _full(shape):
    nd = len(shape)
    return pl.BlockSpec(shape, lambda *_: (0,) * nd)


def _layer(arr, l):
    nd = arr.ndim
    return pl.BlockSpec((1,) + arr.shape[1:], lambda *_: (l,) + (0,) * (nd - 1),
                        pipeline_mode=pl.Buffered(1))


def _cast_specs(w, l, steps):
    _, r, c = w.shape
    k = 1
    while steps % k or r % (steps // k) or (r // (steps // k)) % BF16_SUBLANES:
        k *= 2
        assert k <= steps, (r, steps)
    rows = r // (steps // k)
    in_spec = pl.BlockSpec((1, rows, c), lambda i: (l, i // k, 0))
    out_spec = pl.BlockSpec((1, rows, c), lambda i: (0, i // k, 0))
    return in_spec, out_spec, jax.ShapeDtypeStruct((1, r, c), BF16)


def _rows3(arr):
    return arr.reshape(arr.shape[0], 1, arr.shape[1])


def _shift_rows(x, prev8, d):
    rolled = pltpu.roll(x, d, axis=0)
    prolled = pltpu.roll(prev8, d, axis=0)
    row = lax.broadcasted_iota(jnp.int32, prev8.shape, 0)
    top = jnp.where(row < d, prolled, rolled[:SUBLANES])
    return jnp.concatenate([top, rolled[SUBLANES:]], axis=0)


def _causal_conv(x, prev8, w, b):
    k = w.shape[0]
    y = x * w[k - 1:k]
    for d in range(1, k):
        y = y + _shift_rows(x, prev8, d) * w[k - 1 - d:k - d]
    return y + b


def _rms(x, g):
    return x * lax.rsqrt(jnp.mean(x * x, axis=-1, keepdims=True) + EPS) * g


def _sigmoid(x):
    return 0.5 + 0.5 * jnp.tanh(0.5 * x)


def _silu(x):
    h = 0.5 * x
    return h * (1.0 + jnp.tanh(h))


def _inproj_kernel(x_ref, xn_ref, g_ref, w_ref, wg_ref, wu_ref, wd_ref, wo_ref,
                   z_ref, zg_ref, zgt_ref, wub_ref, wdb_ref, wob_ref, ha_ref, hb_ref):
    half = TM_IN // 2
    g = g_ref[0]
    wub_ref[...] = wu_ref[...].astype(BF16)
    wdb_ref[...] = wd_ref[...].astype(BF16)
    wob_ref[...] = wo_ref[...].astype(BF16)

    @pl.when(pl.program_id(0) == 0)
    def _():
        ha_ref[...] = _rms(x_ref[:half], g).astype(BF16)

    def project(h, rows):
        for c0 in range(0, D_MAIN, TN_IN):
            z_ref[rows, c0:c0 + TN_IN] = jnp.dot(h, w_ref[0, :, c0:c0 + TN_IN], preferred_element_type=F32)
        zg = jnp.dot(h, wg_ref[...], preferred_element_type=F32)
        zg_ref[rows, :] = zg
        zgt_ref[:, rows] = zg.T[:SUBLANES]

    hb_ref[...] = _rms(x_ref[half:], g).astype(BF16)
    project(ha_ref[...], slice(0, half))
    ha_next = _rms(xn_ref[...], g).astype(BF16)
    project(hb_ref[...], slice(half, TM_IN))
    ha_ref[...] = ha_next


def _inproj(x2d, g, w_in_b, w_gate, w_up, w_down, w_out, l):
    t, d = x2d.shape
    half = TM_IN // 2
    last_half = t // half - 1
    steps = t // TM_IN
    casts = [_cast_specs(w, l, steps) for w in (w_up, w_down, w_out)]
    return pl.pallas_call(
        _inproj_kernel,
        grid=(steps,),
        in_specs=[pl.BlockSpec((TM_IN, d), lambda i: (i, 0)),
                  pl.BlockSpec((half, d), lambda i: (jnp.minimum(2 * i + 2, last_half), 0)),
                  _layer(g, l), _layer(w_in_b, l), _full((d, LANES))] + [c[0] for c in casts],
        out_specs=[pl.BlockSpec((TM_IN, D_MAIN), lambda i: (i, 0)),
                   pl.BlockSpec((TM_IN, LANES), lambda i: (i, 0)),
                   pl.BlockSpec((SUBLANES, TM_IN), lambda i: (0, i))] + [c[1] for c in casts],
        out_shape=[jax.ShapeDtypeStruct((t, D_MAIN), F32), jax.ShapeDtypeStruct((t, LANES), F32),
                   jax.ShapeDtypeStruct((SUBLANES, t), F32)] + [c[2] for c in casts],
        scratch_shapes=[pltpu.VMEM((half, d), BF16), pltpu.VMEM((half, d), BF16)],
        compiler_params=_params(1),
        name="inproj",
    )(x2d, x2d, g, w_in_b, w_gate, w_up, w_down, w_out)


def _rglru_tile(x, gate, cw_ref, cb_ref, wa_ref, ba_ref, wx_ref, bx_ref, lam_ref, ng_ref,
                prev_ref, carry_ref, a_ref, b_ref, h_ref):
    tt = x.shape[0]
    n_slabs = x.shape[1] // LANES
    u = _causal_conv(x, prev_ref[...], cw_ref[0], cb_ref[0])
    prev_ref[...] = x[tt - SUBLANES:]
    ub = u.astype(BF16)
    ra, ri = [], []
    for n in range(RG_BLOCKS):
        un = ub[:, n * RG_BLOCK:(n + 1) * RG_BLOCK]
        ra.append(jnp.dot(un, wa_ref[0, n], preferred_element_type=F32))
        ri.append(jnp.dot(un, wx_ref[0, n], preferred_element_type=F32))
    r = _sigmoid(jnp.concatenate(ra, axis=-1) + ba_ref[0])
    i = _sigmoid(jnp.concatenate(ri, axis=-1) + bx_ref[0])
    log_a = r * (RG_C * jax.nn.log_sigmoid(lam_ref[0]))
    a = jnp.exp(log_a)
    gain2 = -jnp.tanh(log_a) * (a * a + 1.0)
    gain = jnp.where(gain2 > 0.0, gain2 * lax.rsqrt(gain2), 0.0)
    bt = gain * (i * u)
    for s in range(n_slabs):
        a_ref[s] = a[:, s * LANES:(s + 1) * LANES]
        b_ref[s] = bt[:, s * LANES:(s + 1) * LANES]

    sub = lax.broadcasted_iota(jnp.int32, (SUBLANES, LANES), 0)
    for s in range(n_slabs):
        lanes = slice(s * LANES, (s + 1) * LANES)
        carry = jnp.broadcast_to(carry_ref[:, lanes], (SUBLANES, LANES))
        for base in range(0, tt, SCAN_BLOCK):
            acc_a, acc_h = [], []
            for k in range(SUBLANES):
                rows = pl.ds(base + k, SUBLANES, stride=SUBLANES)
                ak, bk = a_ref[s, rows, :], b_ref[s, rows, :]
                acc_h.append(bk if k == 0 else ak * acc_h[-1] + bk)
                acc_a.append(ak if k == 0 else ak * acc_a[-1])
            seg_a, seg_h = acc_a[-1], acc_h[-1]
            for d in (1, 2, 4):
                keep = sub >= d
                seg_h = jnp.where(keep, seg_a * pltpu.roll(seg_h, d, axis=0) + seg_h, seg_h)
                seg_a = jnp.where(keep, seg_a * pltpu.roll(seg_a, d, axis=0), seg_a)
            seg_end = seg_a * carry + seg_h
            enter = jnp.where(sub == 0, carry, pltpu.roll(seg_end, 1, axis=0))
            for k in range(SUBLANES):
                h_ref[s, pl.ds(base + k, SUBLANES, stride=SUBLANES), :] = acc_h[k] + acc_a[k] * enter
            carry = jnp.broadcast_to(seg_end[SUBLANES - 1:, :], (SUBLANES, LANES))
        carry_ref[:, lanes] = carry[:1]

    h = jnp.concatenate([h_ref[s] for s in range(n_slabs)], axis=-1)
    y = jax.nn.gelu(gate) * h
    return _rms(y, ng_ref[0]).astype(BF16)


def _rglru_kernel(x_ref, gate_ref, cw_ref, cb_ref, wa_ref, ba_ref, wx_ref, bx_ref, lam_ref, ng_ref,
                  y_ref, prev_ref, carry_ref, a_ref, b_ref, h_ref):
    @pl.when(pl.program_id(1) == 0)
    def _():
        prev_ref[...] = jnp.zeros_like(prev_ref)
        carry_ref[...] = jnp.zeros_like(carry_ref)

    y_ref[0] = _rglru_tile(x_ref[0], gate_ref[0], cw_ref, cb_ref, wa_ref, ba_ref, wx_ref, bx_ref, lam_ref, ng_ref,
                           prev_ref, carry_ref, a_ref, b_ref, h_ref)


def _rglru(z, cw, cb, wa, ba, wx, bx, lam, ng, l):
    b, s, _ = z.shape
    w = RG_WIDTH
    slabs = pltpu.VMEM((w // LANES, TT_RG, LANES), F32)
    return pl.pallas_call(
        _rglru_kernel,
        grid=(b, s // TT_RG),
        in_specs=[pl.BlockSpec((1, TT_RG, w), lambda bi, ti: (bi, ti, 0)),
                  pl.BlockSpec((1, TT_RG, w), lambda bi, ti: (bi, ti, 1)),
                  _layer(cw, l), _layer(cb, l), _layer(wa, l), _layer(ba, l), _layer(wx, l), _layer(bx, l),
                  _layer(lam, l), _layer(ng, l)],
        out_specs=pl.BlockSpec((1, TT_RG, w), lambda bi, ti: (bi, ti, 0)),
        out_shape=jax.ShapeDtypeStruct((b, s, w), BF16),
        scratch_shapes=[pltpu.VMEM((SUBLANES, w), F32), pltpu.VMEM((1, w), F32), slabs, slabs, slabs],
        compiler_params=_params(2),
        name="rglru",
    )(z, z, cw, cb, wa, ba, wx, bx, lam, ng)


def _rope_table_kernel(pos_ref, f_ref, c_ref, s_ref):
    ang = pos_ref[0].astype(F32) * f_ref[...]
    cos, sin = jnp.cos(ang), jnp.sin(ang)
    rest = (DA_HEAD_DIM - ROPE_DIM, ang.shape[1])
    c_map = jnp.concatenate([cos, cos, jnp.ones(rest, F32)], axis=0)
    s_map = jnp.concatenate([-sin, sin, jnp.zeros(rest, F32)], axis=0)
    c_ref[0] = jnp.concatenate([c_map, c_map], axis=0).T
    s_ref[0] = jnp.concatenate([s_map, s_map], axis=0).T


def _rope_tables(positions):
    b, s = positions.shape
    half = ROPE_DIM // 2
    inv_freq = ROPE_THETA ** (-jnp.arange(0, ROPE_DIM, 2, dtype=F32) / ROPE_DIM)
    tab = pl.BlockSpec((1, s, 2 * DA_HEAD_DIM), lambda bi: (bi, 0, 0))
    return pl.pallas_call(
        _rope_table_kernel,
        grid=(b,),
        in_specs=[pl.BlockSpec((1, 1, s), lambda bi: (bi, 0, 0)), _full((half, 1))],
        out_specs=[tab, tab],
        out_shape=[jax.ShapeDtypeStruct((b, s, 2 * DA_HEAD_DIM), F32)] * 2,
        compiler_params=_params(1),
        name="rope_tables",
    )(positions.reshape(b, 1, s), inv_freq.reshape(half, 1))


def _rope(x, c, s):
    half = ROPE_DIM // 2
    lane = lax.broadcasted_iota(jnp.int32, (1, LANES), 1)
    first = (lane % DA_HEAD_DIM) < half
    partner = jnp.where(first, pltpu.roll(x, LANES - half, axis=1), pltpu.roll(x, half, axis=1))
    return x * c + partner * s


def _attn_kernel(lam_ref, q_ref, k_ref, v_ref, cos_ref, sin_ref, ng_ref, o_ref,
                 qb_ref, kz_ref, vb_ref, sc_ref, p_ref, *, lambda_init):
    s_len = k_ref.shape[1]
    nt = s_len // TK
    groups = TK // LANES
    lane = lax.broadcasted_iota(jnp.int32, (1, LANES), 1)
    map0 = lane < DA_HEAD_DIM
    q_scale = DA_HEAD_DIM ** -0.5 * math.log2(math.e)

    for j in range(nt):
        rows = slice(j * TK, (j + 1) * TK)
        cos, sin = cos_ref[0, rows, :], sin_ref[0, rows, :]
        kr = _rope(k_ref[0, rows, :], cos, sin)
        kz_ref[0, rows, :] = jnp.where(map0, kr, 0.0).astype(BF16)
        kz_ref[1, rows, :] = jnp.where(map0, 0.0, kr).astype(BF16)
        vb_ref[rows, :] = jnp.concatenate(
            [v_ref[0, rows, :], jnp.broadcast_to(jnp.where(lane == 0, 1.0, 0.0), (TK, LANES))], axis=-1).astype(BF16)
        qb_ref[rows, :] = (_rope(q_ref[0, rows, :], cos, sin) * q_scale).astype(BF16)

    lp = lam_ref[0]
    lam = (jnp.exp(jnp.sum(lp[0:1] * lp[1:2], axis=-1, keepdims=True))
           - jnp.exp(jnp.sum(lp[2:3] * lp[3:4], axis=-1, keepdims=True)) + lambda_init)
    on_or_below_diag = (lax.broadcasted_iota(jnp.int32, (TQ, TK), 0)
                        >= lax.broadcasted_iota(jnp.int32, (TQ, TK), 1))

    for qi in reversed(range(nt)):
        kv = (qi + 1) * TK
        q = qb_ref[qi * TQ:(qi + 1) * TQ, :]
        for c in range(2):
            sc_ref[c, :, :kv] = lax.dot_general(q, kz_ref[c, :kv, :], (((1,), (1,)), ((), ())),
                                                preferred_element_type=F32)
        for c in range(2):
            m = None
            for j in range(qi + 1):
                cols = slice(j * TK, (j + 1) * TK)
                s = sc_ref[c, :, cols]
                if j == qi:
                    s = jnp.where(on_or_below_diag, s, NEG_INF)
                    sc_ref[c, :, cols] = s
                for g in range(groups):
                    sg = s[:, g * LANES:(g + 1) * LANES]
                    m = sg if m is None else jnp.maximum(m, sg)
            m_b = jnp.broadcast_to(jnp.max(m, axis=-1, keepdims=True), (TQ, LANES))
            m_b = jnp.concatenate([m_b] * groups, axis=-1)
            for j in range(qi + 1):
                cols = slice(j * TK, (j + 1) * TK)
                p_ref[c * TQ:(c + 1) * TQ, cols] = jnp.exp2(sc_ref[c, :, cols] - m_b).astype(BF16)
        acc = jnp.dot(p_ref[:, :kv], vb_ref[:kv, :], preferred_element_type=F32)
        hw = 2 * DA_HEAD_DIM
        o = (acc[:TQ, :hw] / acc[:TQ, hw:hw + 1]) - lam * (acc[TQ:, :hw] / acc[TQ:, hw:hw + 1])
        o_ref[0, qi * TQ:(qi + 1) * TQ, :] = (_rms(o, ng_ref[0]) * (1.0 - lambda_init)).astype(BF16)


def _attention(z, cos_t, sin_t, lam_p, ng, lambda_init, l):
    b, s, _ = z.shape
    hw = 2 * DA_HEAD_DIM
    q0 = (2 * RG_WIDTH) // hw
    k0 = q0 + DA_HEADS
    v0 = k0 + DA_HEADS
    seq = lambda c0: pl.BlockSpec((1, s, hw), lambda bi, hi: (bi, 0, c0 + hi))
    tab = pl.BlockSpec((1, s, hw), lambda bi, hi: (bi, 0, 0))
    return pl.pallas_call(
        functools.partial(_attn_kernel, lambda_init=lambda_init),
        grid=(b, DA_HEADS),
        in_specs=[_layer(lam_p, l), seq(q0), seq(k0), seq(v0), tab, tab, _layer(ng, l)],
        out_specs=pl.BlockSpec((1, s, hw), lambda bi, hi: (bi, 0, hi)),
        out_shape=jax.ShapeDtypeStruct((b, s, DA_WIDTH), BF16),
        scratch_shapes=[pltpu.VMEM((s, hw), BF16), pltpu.VMEM((2, s, hw), BF16), pltpu.VMEM((s, 2 * hw), BF16),
                        pltpu.VMEM((2, TQ, s), F32), pltpu.VMEM((2 * TQ, s), BF16)],
        compiler_params=_params(2),
        name="diffattn",
    )(lam_p, z, z, z, cos_t, sin_t, ng)


def _mlstm_kernel(q_ref, k_ref, v_ref, o_ref, gcol_ref, grow_ref, cw_ref, cb_ref, brow_ref, bcol_ref,
                  ng_ref, y_ref, pq_ref, pk_ref, c_ref, n_ref, m_ref):
    @pl.when(pl.program_id(1) == 0)
    def _():
        pq_ref[...] = jnp.zeros_like(pq_ref)
        pk_ref[...] = jnp.zeros_like(pk_ref)
        c_ref[...] = jnp.zeros_like(c_ref)
        n_ref[...] = jnp.zeros_like(n_ref)
        m_ref[...] = jnp.zeros_like(m_ref)

    q_raw, k_raw = q_ref[0], k_ref[0]
    y_ref[0] = _mlstm_chunk(q_raw, k_raw, pq_ref[...], pk_ref[...], v_ref[0], o_ref[0], gcol_ref[0],
                            grow_ref[...], cw_ref, cb_ref, brow_ref, bcol_ref, ng_ref, c_ref, n_ref, m_ref)
    pq_ref[...] = q_raw[L_ML - SUBLANES:]
    pk_ref[...] = k_raw[L_ML - SUBLANES:]


def _mlstm_chunk(q_raw, k_raw, pq, pk, vv, o_pre, gcol, grow, cw_ref, cb_ref, brow_ref, bcol_ref, ng_ref,
                 c_ref, n_ref, m_ref):
    w = ML_WIDTH
    dh = ML_HEAD_DIM
    ln = q_raw.shape[0]
    qs = _silu(_causal_conv(q_raw, pq, cw_ref[0, :, :w], cb_ref[0, :, :w]))
    ks = _silu(_causal_conv(k_raw, pk, cw_ref[0, :, w:], cb_ref[0, :, w:])) * (dh ** -0.5)
    og = _sigmoid(o_pre)

    gc = gcol + brow_ref[...]
    gr = grow + bcol_ref[...]
    r_i = lax.broadcasted_iota(jnp.int32, (ln, ln), 0)
    c_i = lax.broadcasted_iota(jnp.int32, (ln, ln), 1)
    causal = r_i >= c_i
    b_cols = jnp.dot(causal.astype(F32), jax.nn.log_sigmoid(gc),
                     precision=lax.Precision.HIGHEST, preferred_element_type=F32)
    b_rows = jnp.dot(jax.nn.log_sigmoid(gr), (r_i <= c_i).astype(F32),
                     precision=lax.Precision.HIGHEST, preferred_element_type=F32)

    lane = lax.broadcasted_iota(jnp.int32, (1, LANES), 1)
    sub = lax.broadcasted_iota(jnp.int32, (SUBLANES, 1), 0)
    t_col = lax.broadcasted_iota(jnp.int32, (ln, 1), 0)

    def lane_pick(a, j):
        return jnp.sum(jnp.where(lane == j, a, 0.0), axis=-1, keepdims=True)

    def row_pick(a, j):
        return jnp.sum(jnp.where(sub == j, a, 0.0), axis=0, keepdims=True)

    outs = []
    for hh in range(ML_HEADS):
        sl = slice(hh * dh, (hh + 1) * dh)
        b_col = lane_pick(b_cols, ML_HEADS + hh)
        li_col = lane_pick(gc, hh)
        r_row = row_pick(gr, hh) - row_pick(b_rows, ML_HEADS + hh)
        m_prev = m_ref[hh]
        dmat = jnp.where(causal, b_col + r_row, -jnp.inf)
        m_inter = b_col + m_prev
        m_t = jnp.maximum(m_inter, jnp.max(dmat, axis=-1, keepdims=True))
        qh = qs[:, sl]
        kh = ks[:, sl]
        qb = qh.astype(BF16)
        vb = vv[:, sl].astype(BF16)
        w_intra = lax.dot_general(qb, kh.astype(BF16), (((1,), (1,)), ((), ())),
                                  preferred_element_type=F32) * jnp.exp(dmat - m_t)
        inter = jnp.exp(m_inter - m_t)
        c_prev = c_ref[hh]
        n_prev = n_ref[hh]
        num = (inter * jnp.dot(qb, c_prev.astype(BF16), preferred_element_type=F32)
               + jnp.dot(w_intra.astype(BF16), vb, preferred_element_type=F32))
        den = (inter * jnp.sum(qh * n_prev, axis=-1, keepdims=True)
               + jnp.sum(w_intra, axis=-1, keepdims=True))
        hout = num / jnp.maximum(jnp.abs(den), jnp.exp(-m_t))
        outs.append(_rms(hout, ng_ref[0, :, sl]))
        b_last = jnp.sum(jnp.where(t_col == ln - 1, b_col, 0.0), axis=0, keepdims=True)
        g_col = b_last - b_col + li_col
        m_next = jnp.maximum(b_last + m_prev, jnp.max(g_col, axis=0, keepdims=True))
        decay = jnp.exp(b_last + m_prev - m_next)
        kw = kh * jnp.exp(g_col - m_next)
        c_ref[hh] = decay * c_prev + lax.dot_general(kw.astype(BF16), vb, (((0,), (0,)), ((), ())),
                                                     preferred_element_type=F32)
        n_ref[hh] = decay * n_prev + jnp.sum(kw, axis=0, keepdims=True)
        m_ref[hh] = m_next
    return (jnp.concatenate(outs, axis=-1) * og).astype(BF16)


def _mlstm(z, zg, zgt, cw, cb, brow, bcol, ng, l):
    b, s, _ = z.shape
    w = ML_WIDTH
    nc = s // L_ML
    c0 = ML_Q0 // w
    col = lambda c: pl.BlockSpec((1, L_ML, w), lambda bi, ci: (bi, ci, c))
    return pl.pallas_call(
        _mlstm_kernel,
        grid=(b, nc),
        in_specs=[col(c0), col(c0 + 1), col(c0 + 2), col(c0 + 3),
                  pl.BlockSpec((1, L_ML, LANES), lambda bi, ci: (bi, ci, 0)),
                  pl.BlockSpec((SUBLANES, L_ML), lambda bi, ci: (0, bi * nc + ci)),
                  _layer(cw, l), _layer(cb, l), _full((1, LANES)), _full((SUBLANES, 1)), _layer(ng, l)],
        out_specs=col(0),
        out_shape=jax.ShapeDtypeStruct((b, s, w), BF16),
        scratch_shapes=[pltpu.VMEM((SUBLANES, w), F32), pltpu.VMEM((SUBLANES, w), F32),
                        pltpu.VMEM((ML_HEADS, ML_HEAD_DIM, ML_HEAD_DIM), F32),
                        pltpu.VMEM((ML_HEADS, 1, ML_HEAD_DIM), F32),
                        pltpu.VMEM((ML_HEADS, 1, 1), F32)],
        compiler_params=_params(2),
        name="mlstm",
    )(z, z, z, z, zg, zgt, cw, cb, brow, bcol, ng)


def _outproj_kernel(x_ref, yr_ref, ya_ref, ym_ref, w_ref, g_ref, x1_ref, h_ref):
    half = TM_OUT // 2
    for r0 in (0, half):
        rows = slice(r0, r0 + half)
        acc = x_ref[rows, :]
        acc = acc + jnp.dot(yr_ref[rows, :], w_ref[0, 0:RG_WIDTH, :], preferred_element_type=F32)
        acc = acc + jnp.dot(ya_ref[rows, :], w_ref[0, RG_WIDTH:RG_WIDTH + DA_WIDTH, :], preferred_element_type=F32)
        acc = acc + jnp.dot(ym_ref[rows, :], w_ref[0, RG_WIDTH + DA_WIDTH:, :], preferred_element_type=F32)
        x1_ref[rows, :] = acc
        h_ref[rows, :] = _rms(acc, g_ref[0]).astype(BF16)


def _outproj(x2d, y_rg, y_da, y_ml, w_out, g, l):
    t, d = x2d.shape
    row = lambda wd: pl.BlockSpec((TM_OUT, wd), lambda i: (i, 0))
    return pl.pallas_call(
        _outproj_kernel,
        grid=(t // TM_OUT,),
        in_specs=[row(d), row(RG_WIDTH), row(DA_WIDTH), row(ML_WIDTH), _layer(w_out, 0), _layer(g, l)],
        out_specs=[row(d), row(d)],
        out_shape=[jax.ShapeDtypeStruct((t, d), F32), jax.ShapeDtypeStruct((t, d), BF16)],
        compiler_params=_params(1),
        name="outproj",
    )(x2d, y_rg, y_da, y_ml, w_out, g)


def _mlp_kernel(h_ref, x_ref, wup_ref, cw_ref, cb_ref, wdn_ref, ng_ref, o_ref, tail_ref, u_ref, act_ref,
                hs_ref, *, final_norm):
    @pl.when(pl.program_id(1) == 0)
    def _():
        tail_ref[...] = jnp.zeros_like(tail_ref)

    tm = h_ref.shape[1]
    hs_ref[...] = h_ref[0]
    n_ch = D_FF // CH_FF
    assert n_ch % 2 == 1 and n_ch * CH_FF == D_FF

    def cols(half, j):
        return pl.ds(half * D_FF + j * CH_FF, CH_FF)

    def up(j, slot):
        for half in range(2):
            u_ref[slot, half] = jnp.dot(hs_ref[...], wup_ref[0, :, cols(half, j)], preferred_element_type=F32)

    def gate(j, slot):
        out_cols = pl.ds(j * CH_FF, CH_FF)
        taps = [(cw_ref[0, :, cols(half, j)] * sc, cb_ref[0, :, cols(half, j)] * sc) for half, sc in ((0, 0.5), (1, 1.0))]
        for r0 in range(0, tm, RB_FF):
            halves = []
            for half in range(2):
                prev = tail_ref[:, cols(half, j)] if r0 == 0 else u_ref[slot, half, r0 - SUBLANES:r0, :]
                halves.append(_causal_conv(u_ref[slot, half, r0:r0 + RB_FF, :], prev, *taps[half]))
            g_half = halves[0]
            act_ref[r0:r0 + RB_FF, out_cols] = (g_half * (1.0 + jnp.tanh(g_half)) * halves[1]).astype(BF16)
        for half in range(2):
            tail_ref[:, cols(half, j)] = u_ref[slot, half, tm - SUBLANES:, :]

    up(0, 0)
    for j in range(n_ch):
        if j + 1 < n_ch:
            up(j + 1, (j + 1) % 2)
        gate(j, j % 2)

    y = x_ref[0] + jnp.dot(act_ref[...], wdn_ref[0], preferred_element_type=F32)
    if final_norm:
        y = _rms(y, ng_ref[...])
    o_ref[0] = y


def _mlp(h, x1, w_up_b, cw, cb, w_down_b, ng, final_norm, l):
    b, s, d = x1.shape
    tile = pl.BlockSpec((1, TM_MLP, d), lambda bi, ti: (bi, ti, 0))
    return pl.pallas_call(
        functools.partial(_mlp_kernel, final_norm=final_norm),
        grid=(b, s // TM_MLP),
        in_specs=[tile, tile, _layer(w_up_b, 0), _layer(cw, l), _layer(cb, l), _layer(w_down_b, 0), _full((1, d))],
        out_specs=tile,
        out_shape=jax.ShapeDtypeStruct((b, s, d), F32),
        scratch_shapes=[pltpu.VMEM((SUBLANES, 2 * D_FF), F32), pltpu.VMEM((2, 2, TM_MLP, CH_FF), F32),
                        pltpu.VMEM((TM_MLP, D_FF), BF16), pltpu.VMEM((TM_MLP, d), BF16)],
        compiler_params=_params(2),
        name="mlp",
    )(h, x1, w_up_b, cw, cb, w_down_b, ng)


def kernel(x, positions, attn_norm, w_in, rg_conv_w, rg_conv_b, rg_wa, rg_ba, rg_wx, rg_bx, rg_lambda, rg_norm, da_lambda, da_norm, ml_conv_w, ml_conv_b, ml_i_bias, ml_f_bias, ml_norm, w_out, mlp_norm, w_up, ffn_conv_w, ffn_conv_b, w_down, final_norm):
    b, s, d = x.shape
    depth = w_in.shape[0]
    t = b * s
    n_gate = 2 * ML_HEADS
    cos_t, sin_t = _rope_tables(positions)
    w_in_b, wa_b, wx_b = w_in.astype(BF16), rg_wa.astype(BF16), rg_wx.astype(BF16)
    w_gate_b = jnp.pad(w_in[:, :, D_MAIN:], ((0, 0), (0, 0), (0, LANES - n_gate))).astype(BF16)
    gate_bias = jnp.concatenate([ml_i_bias, ml_f_bias], axis=-1)
    attn_norm3, mlp_norm3, rg_norm3, da_norm3, ml_norm3 = map(_rows3, (attn_norm, mlp_norm, rg_norm, da_norm, ml_norm))
    rg_cb3, rg_ba3, rg_bx3, rg_lam3, ml_cb3, ffn_cb3 = map(_rows3, (rg_conv_b, rg_ba, rg_bx, rg_lambda, ml_conv_b, ffn_conv_b))
    for l in range(depth):
        lambda_init = 0.8 - 0.6 * math.exp(-0.3 * l)
        z, zg, zgt, w_up_b, w_down_b, w_out_b = _inproj(x.reshape(t, d), attn_norm3, w_in_b, w_gate_b[l],
                                                        w_up, w_down, w_out, l)
        z = z.reshape(b, s, D_MAIN)
        zg = zg.reshape(b, s, LANES)
        y_rg = _rglru(z, rg_conv_w, rg_cb3, wa_b, rg_ba3, wx_b, rg_bx3, rg_lam3, rg_norm3, l)
        y_da = _attention(z, cos_t, sin_t, da_lambda, da_norm3, lambda_init, l)
        y_ml = _mlstm(z, zg, zgt, ml_conv_w, ml_cb3,
                      jnp.pad(gate_bias[l], (0, LANES - n_gate)).reshape(1, LANES),
                      gate_bias[l].reshape(n_gate, 1), ml_norm3, l)
        x1, h2 = _outproj(x.reshape(t, d), y_rg.reshape(t, -1), y_da.reshape(t, -1), y_ml.reshape(t, -1),
                          w_out_b, mlp_norm3, l)
        x = _mlp(h2.reshape(b, s, d), x1.reshape(b, s, d), w_up_b, ffn_conv_w, ffn_cb3, w_down_b,
                 final_norm.reshape(1, d), l == depth - 1, l)
    return x
```

```python
import functools
import math

import jax
import jax.numpy as jnp
from jax import lax
from jax.experimental import pallas as pl
from jax.experimental.pallas import tpu as pltpu

F32 = jnp.float32
BF16 = jnp.bfloat16

D_MODEL = 1024
RG_WIDTH = 512
RG_BLOCKS = 4
RG_BLOCK = RG_WIDTH // RG_BLOCKS
RG_CONV = 4
RG_C = 8.0
DA_HEADS = 4
DA_HEAD_DIM = 64
DA_WIDTH = DA_HEADS * 2 * DA_HEAD_DIM
ROPE_THETA = 500000.0
ROPE_DIM = DA_HEAD_DIM // 4
NEG_INF = -1e30
ML_HEADS = 4
ML_HEAD_DIM = 128
ML_WIDTH = ML_HEADS * ML_HEAD_DIM
ML_CONV = 4
D_MIX = RG_WIDTH + DA_WIDTH + ML_WIDTH
D_FF = 2816
FFN_CONV = 3
EPS = 1e-6
IN_WIDTHS = (RG_WIDTH, RG_WIDTH, DA_WIDTH, DA_WIDTH, DA_WIDTH,
             ML_WIDTH, ML_WIDTH, ML_WIDTH, ML_WIDTH, ML_HEADS, ML_HEADS)
D_IN = sum(IN_WIDTHS)

LANES = 128
SUBLANES = 8
BF16_SUBLANES = 16
D_MAIN = D_IN - 2 * ML_HEADS
ML_Q0 = 2 * RG_WIDTH + 3 * DA_WIDTH
VMEM_LIMIT = 56 * 1024 * 1024

TM_IN = 512
TN_IN = 1536
TM_OUT = 1024
TT_RG = 256
SCAN_BLOCK = SUBLANES * SUBLANES
TQ = 256
TK = 256
HEADS_PER_STEP = 2
L_ML = 256
TM_MLP = 1024
CH_FF = 256
RB_FF = 128


def _params(n_axes):
    return pltpu.CompilerParams(dimension_semantics=("arbitrary",) * n_axes,
                                vmem_limit_bytes=VMEM_LIMIT)


def _full(shape):
    nd = len(shape)
    return pl.BlockSpec(shape, lambda *_: (0,) * nd)


def _layer(arr, l):
    nd = arr.ndim
    return pl.BlockSpec((1,) + arr.shape[1:], lambda *_: (l,) + (0,) * (nd - 1),
                        pipeline_mode=pl.Buffered(1))


def _cast_specs(w, l, steps):
    _, r, c = w.shape
    k = 1
    while steps % k or r % (steps // k) or (r // (steps // k)) % BF16_SUBLANES:
        k *= 2
        assert k <= steps, (r, steps)
    rows = r // (steps // k)
    in_spec = pl.BlockSpec((1, rows, c), lambda i: (l, i // k, 0))
    out_spec = pl.BlockSpec((1, rows, c), lambda i: (0, i // k, 0))
    return in_spec, out_spec, jax.ShapeDtypeStruct((1, r, c), BF16)


def _rows3(arr):
    return arr.reshape(arr.shape[0], 1, arr.shape[1])


def _shift_rows(x, prev8, d):
    rolled = pltpu.roll(x, d, axis=0)
    prolled = pltpu.roll(prev8, d, axis=0)
    row = lax.broadcasted_iota(jnp.int32, prev8.shape, 0)
    top = jnp.where(row < d, prolled, rolled[:SUBLANES])
    return jnp.concatenate([top, rolled[SUBLANES:]], axis=0)


def _causal_conv(x, prev8, w, b):
    k = w.shape[0]
    y = x * w[k - 1:k]
    for d in range(1, k):
        y = y + _shift_rows(x, prev8, d) * w[k - 1 - d:k - d]
    return y + b


def _rms(x, g):
    return x * lax.rsqrt(jnp.mean(x * x, axis=-1, keepdims=True) + EPS) * g


def _sigmoid(x):
    return 0.5 + 0.5 * jnp.tanh(0.5 * x)


def _silu(x):
    h = 0.5 * x
    return h * (1.0 + jnp.tanh(h))


def _inproj_kernel(x_ref, xn_ref, g_ref, w_ref, wg_ref, wu_ref, wd_ref, wo_ref,
                   z_ref, zg_ref, zgt_ref, wub_ref, wdb_ref, wob_ref, ha_ref, hb_ref):
    half = TM_IN // 2
    g = g_ref[0]
    wub_ref[...] = wu_ref[...].astype(BF16)
    wdb_ref[...] = wd_ref[...].astype(BF16)
    wob_ref[...] = wo_ref[...].astype(BF16)

    @pl.when(pl.program_id(0) == 0)
    def _():
        ha_ref[...] = _rms(x_ref[:half], g).astype(BF16)

    def project(h, rows):
        for c0 in range(0, D_MAIN, TN_IN):
            z_ref[rows, c0:c0 + TN_IN] = jnp.dot(h, w_ref[0, :, c0:c0 + TN_IN], preferred_element_type=F32)
        zg = jnp.dot(h, wg_ref[...], preferred_element_type=F32)
        zg_ref[rows, :] = zg
        zgt_ref[:, rows] = zg.T[:SUBLANES]

    hb_ref[...] = _rms(x_ref[half:], g).astype(BF16)
    project(ha_ref[...], slice(0, half))
    ha_next = _rms(xn_ref[...], g).astype(BF16)
    project(hb_ref[...], slice(half, TM_IN))
    ha_ref[...] = ha_next


def _inproj(x2d, g, w_in_b, w_gate, w_up, w_down, w_out, l):
    t, d = x2d.shape
    half = TM_IN // 2
    last_half = t // half - 1
    steps = t // TM_IN
    casts = [_cast_specs(w, l, steps) for w in (w_up, w_down, w_out)]
    return pl.pallas_call(
        _inproj_kernel,
        grid=(steps,),
        in_specs=[pl.BlockSpec((TM_IN, d), lambda i: (i, 0)),
                  pl.BlockSpec((half, d), lambda i: (jnp.minimum(2 * i + 2, last_half), 0)),
                  _layer(g, l), _layer(w_in_b, l), _full((d, LANES))] + [c[0] for c in casts],
        out_specs=[pl.BlockSpec((TM_IN, D_MAIN), lambda i: (i, 0)),
                   pl.BlockSpec((TM_IN, LANES), lambda i: (i, 0)),
                   pl.BlockSpec((SUBLANES, TM_IN), lambda i: (0, i))] + [c[1] for c in casts],
        out_shape=[jax.ShapeDtypeStruct((t, D_MAIN), F32), jax.ShapeDtypeStruct((t, LANES), F32),
                   jax.ShapeDtypeStruct((SUBLANES, t), F32)] + [c[2] for c in casts],
        scratch_shapes=[pltpu.VMEM((half, d), BF16), pltpu.VMEM((half, d), BF16)],
        compiler_params=_params(1),
        name="inproj",
    )(x2d, x2d, g, w_in_b, w_gate, w_up, w_down, w_out)


def _rglru_tile(x, gate, cw_ref, cb_ref, wa_ref, ba_ref, wx_ref, bx_ref, lam_ref, ng_ref,
                prev_ref, carry_ref, a_ref, b_ref, h_ref):
    tt = x.shape[0]
    n_slabs = x.shape[1] // LANES
    u = _causal_conv(x, prev_ref[...], cw_ref[0], cb_ref[0])
    prev_ref[...] = x[tt - SUBLANES:]
    ub = u.astype(BF16)
    ra, ri = [], []
    for n in range(RG_BLOCKS):
        un = ub[:, n * RG_BLOCK:(n + 1) * RG_BLOCK]
        ra.append(jnp.dot(un, wa_ref[0, n], preferred_element_type=F32))
        ri.append(jnp.dot(un, wx_ref[0, n], preferred_element_type=F32))
    r = _sigmoid(jnp.concatenate(ra, axis=-1) + ba_ref[0])
    i = _sigmoid(jnp.concatenate(ri, axis=-1) + bx_ref[0])
    log_a = r * (RG_C * jax.nn.log_sigmoid(lam_ref[0]))
    a = jnp.exp(log_a)
    gain2 = -jnp.tanh(log_a) * (a * a + 1.0)
    gain = jnp.where(gain2 > 0.0, gain2 * lax.rsqrt(gain2), 0.0)
    bt = gain * (i * u)
    for s in range(n_slabs):
        a_ref[s] = a[:, s * LANES:(s + 1) * LANES]
        b_ref[s] = bt[:, s * LANES:(s + 1) * LANES]

    sub = lax.broadcasted_iota(jnp.int32, (SUBLANES, LANES), 0)
    for s in range(n_slabs):
        lanes = slice(s * LANES, (s + 1) * LANES)
        carry = jnp.broadcast_to(carry_ref[:, lanes], (SUBLANES, LANES))
        for base in range(0, tt, SCAN_BLOCK):
            acc_a, acc_h = [], []
            for k in range(SUBLANES):
                rows = pl.ds(base + k, SUBLANES, stride=SUBLANES)
                ak, bk = a_ref[s, rows, :], b_ref[s, rows, :]
                acc_h.append(bk if k == 0 else ak * acc_h[-1] + bk)
                acc_a.append(ak if k == 0 else ak * acc_a[-1])
            seg_a, seg_h = acc_a[-1], acc_h[-1]
            for d in (1, 2, 4):
                keep = sub >= d
                seg_h = jnp.where(keep, seg_a * pltpu.roll(seg_h, d, axis=0) + seg_h, seg_h)
                seg_a = jnp.where(keep, seg_a * pltpu.roll(seg_a, d, axis=0), seg_a)
            seg_end = seg_a * carry + seg_h
            enter = jnp.where(sub == 0, carry, pltpu.roll(seg_end, 1, axis=0))
            for k in range(SUBLANES):
                h_ref[s, pl.ds(base + k, SUBLANES, stride=SUBLANES), :] = acc_h[k] + acc_a[k] * enter
            carry = jnp.broadcast_to(seg_end[SUBLANES - 1:, :], (SUBLANES, LANES))
        carry_ref[:, lanes] = carry[:1]

    h = jnp.concatenate([h_ref[s] for s in range(n_slabs)], axis=-1)
    k1 = math.sqrt(2.0 / math.pi)
    y = (0.5 * gate) * (1.0 + jnp.tanh(gate * (k1 + (k1 * 0.044715) * (gate * gate)))) * h
    return _rms(y, ng_ref[0]).astype(BF16)


def _rglru_kernel(x_ref, gate_ref, cw_ref, cb_ref, wa_ref, ba_ref, wx_ref, bx_ref, lam_ref, ng_ref,
                  y_ref, prev_ref, carry_ref, a_ref, b_ref, h_ref):
    @pl.when(pl.program_id(1) == 0)
    def _():
        prev_ref[...] = jnp.zeros_like(prev_ref)
        carry_ref[...] = jnp.zeros_like(carry_ref)

    y_ref[0] = _rglru_tile(x_ref[0], gate_ref[0], cw_ref, cb_ref, wa_ref, ba_ref, wx_ref, bx_ref, lam_ref, ng_ref,
                           prev_ref, carry_ref, a_ref, b_ref, h_ref)


def _rglru(z, cw, cb, wa, ba, wx, bx, lam, ng, l):
    b, s, _ = z.shape
    w = RG_WIDTH
    slabs = pltpu.VMEM((w // LANES, TT_RG, LANES), F32)
    return pl.pallas_call(
        _rglru_kernel,
        grid=(b, s // TT_RG),
        in_specs=[pl.BlockSpec((1, TT_RG, w), lambda bi, ti: (bi, ti, 0)),
                  pl.BlockSpec((1, TT_RG, w), lambda bi, ti: (bi, ti, 1)),
                  _layer(cw, l), _layer(cb, l), _layer(wa, l), _layer(ba, l), _layer(wx, l), _layer(bx, l),
                  _layer(lam, l), _layer(ng, l)],
        out_specs=pl.BlockSpec((1, TT_RG, w), lambda bi, ti: (bi, ti, 0)),
        out_shape=jax.ShapeDtypeStruct((b, s, w), BF16),
        scratch_shapes=[pltpu.VMEM((SUBLANES, w), F32), pltpu.VMEM((1, w), F32), slabs, slabs, slabs],
        compiler_params=_params(2),
        name="rglru",
    )(z, z, cw, cb, wa, ba, wx, bx, lam, ng)


def _rope_table_kernel(pos_ref, f_ref, c_ref, s_ref):
    ang = pos_ref[0].astype(F32) * f_ref[...]
    cos, sin = jnp.cos(ang), jnp.sin(ang)
    rest = (DA_HEAD_DIM - ROPE_DIM, ang.shape[1])
    c_map = jnp.concatenate([cos, cos, jnp.ones(rest, F32)], axis=0)
    s_map = jnp.concatenate([-sin, sin, jnp.zeros(rest, F32)], axis=0)
    c_ref[0] = jnp.concatenate([c_map, c_map], axis=0).T
    s_ref[0] = jnp.concatenate([s_map, s_map], axis=0).T


def _rope_tables(positions):
    b, s = positions.shape
    half = ROPE_DIM // 2
    inv_freq = ROPE_THETA ** (-jnp.arange(0, ROPE_DIM, 2, dtype=F32) / ROPE_DIM)
    tab = pl.BlockSpec((1, s, 2 * DA_HEAD_DIM), lambda bi: (bi, 0, 0))
    return pl.pallas_call(
        _rope_table_kernel,
        grid=(b,),
        in_specs=[pl.BlockSpec((1, 1, s), lambda bi: (bi, 0, 0)), _full((half, 1))],
        out_specs=[tab, tab],
        out_shape=[jax.ShapeDtypeStruct((b, s, 2 * DA_HEAD_DIM), F32)] * 2,
        compiler_params=_params(1),
        name="rope_tables",
    )(positions.reshape(b, 1, s), inv_freq.reshape(half, 1))


def _rope(x, c, s):
    half = ROPE_DIM // 2
    lane = lax.broadcasted_iota(jnp.int32, (1, LANES), 1)
    first = (lane % DA_HEAD_DIM) < half
    partner = jnp.where(first, pltpu.roll(x, LANES - half, axis=1), pltpu.roll(x, half, axis=1))
    return x * c + partner * s


def _attn_kernel(lam_ref, q_ref, k_ref, v_ref, cos_ref, sin_ref, ng_ref, o_ref,
                 qb_ref, kz_ref, vb_ref, sc_ref, p_ref, *, lambda_init):
    hw = 2 * DA_HEAD_DIM
    for hh in range(q_ref.shape[2] // hw):
        lanes = pl.ds(hh * hw, hw)
        _attn_head(lam_ref, q_ref.at[:, :, lanes], k_ref.at[:, :, lanes], v_ref.at[:, :, lanes], cos_ref, sin_ref,
                   ng_ref, o_ref.at[:, :, lanes], qb_ref.at[hh], kz_ref.at[hh], vb_ref.at[hh], sc_ref.at[hh],
                   p_ref.at[hh], lambda_init=lambda_init)


def _attn_head(lam_ref, q_ref, k_ref, v_ref, cos_ref, sin_ref, ng_ref, o_ref,
               qb_ref, kz_ref, vb_ref, sc_ref, p_ref, *, lambda_init):
    s_len = k_ref.shape[1]
    nt = s_len // TK
    groups = TK // LANES
    lane = lax.broadcasted_iota(jnp.int32, (1, LANES), 1)
    map0 = lane < DA_HEAD_DIM
    q_scale = DA_HEAD_DIM ** -0.5 * math.log2(math.e)

    for j in range(nt):
        rows = slice(j * TK, (j + 1) * TK)
        cos, sin = cos_ref[0, rows, :], sin_ref[0, rows, :]
        kr = _rope(k_ref[0, rows, :], cos, sin)
        kz_ref[0, rows, :] = jnp.where(map0, kr, 0.0).astype(BF16)
        kz_ref[1, rows, :] = jnp.where(map0, 0.0, kr).astype(BF16)
        vb_ref[rows, :] = jnp.concatenate(
            [v_ref[0, rows, :], jnp.broadcast_to(jnp.where(lane == 0, 1.0, 0.0), (TK, LANES))], axis=-1).astype(BF16)
        qb_ref[rows, :] = (_rope(q_ref[0, rows, :], cos, sin) * q_scale).astype(BF16)

    lp = lam_ref[0]
    lam = (jnp.exp(jnp.sum(lp[0:1] * lp[1:2], axis=-1, keepdims=True))
           - jnp.exp(jnp.sum(lp[2:3] * lp[3:4], axis=-1, keepdims=True)) + lambda_init)
    on_or_below_diag = (lax.broadcasted_iota(jnp.int32, (TQ, TK), 0)
                        >= lax.broadcasted_iota(jnp.int32, (TQ, TK), 1))

    for qi in reversed(range(nt)):
        kv = (qi + 1) * TK
        q = qb_ref[qi * TQ:(qi + 1) * TQ, :]
        for c in range(2):
            sc_ref[c, :, :kv] = lax.dot_general(q, kz_ref[c, :kv, :], (((1,), (1,)), ((), ())),
                                                preferred_element_type=F32)
        for c in range(2):
            m = None
            for j in range(qi + 1):
                cols = slice(j * TK, (j + 1) * TK)
                s = sc_ref[c, :, cols]
                if j == qi:
                    s = jnp.where(on_or_below_diag, s, NEG_INF)
                    sc_ref[c, :, cols] = s
                for g in range(groups):
                    sg = s[:, g * LANES:(g + 1) * LANES]
                    m = sg if m is None else jnp.maximum(m, sg)
            m_b = jnp.broadcast_to(jnp.max(m, axis=-1, keepdims=True), (TQ, LANES))
            m_b = jnp.concatenate([m_b] * groups, axis=-1)
            for j in range(qi + 1):
                cols = slice(j * TK, (j + 1) * TK)
                p_ref[c * TQ:(c + 1) * TQ, cols] = jnp.exp2(sc_ref[c, :, cols] - m_b).astype(BF16)
        acc = jnp.dot(p_ref[:, :kv], vb_ref[:kv, :], preferred_element_type=F32)
        hw = 2 * DA_HEAD_DIM
        o = (acc[:TQ, :hw] / acc[:TQ, hw:hw + 1]) - lam * (acc[TQ:, :hw] / acc[TQ:, hw:hw + 1])
        o_ref[0, qi * TQ:(qi + 1) * TQ, :] = (_rms(o, ng_ref[0]) * (1.0 - lambda_init)).astype(BF16)


def _attention(z, cos_t, sin_t, lam_p, ng, lambda_init, l):
    b, s, _ = z.shape
    hw = 2 * DA_HEAD_DIM
    q0 = (2 * RG_WIDTH) // hw
    k0 = q0 + DA_HEADS
    v0 = k0 + DA_HEADS
    nh = HEADS_PER_STEP
    assert q0 % nh == 0 and DA_HEADS % nh == 0
    seq = lambda c0: pl.BlockSpec((1, s, nh * hw), lambda bi, hi: (bi, 0, c0 // nh + hi))
    tab = pl.BlockSpec((1, s, hw), lambda bi, hi: (bi, 0, 0))
    return pl.pallas_call(
        functools.partial(_attn_kernel, lambda_init=lambda_init),
        grid=(b, DA_HEADS // nh),
        in_specs=[_layer(lam_p, l), seq(q0), seq(k0), seq(v0), tab, tab, _layer(ng, l)],
        out_specs=pl.BlockSpec((1, s, nh * hw), lambda bi, hi: (bi, 0, hi)),
        out_shape=jax.ShapeDtypeStruct((b, s, DA_WIDTH), BF16),
        scratch_shapes=[pltpu.VMEM((nh, s, hw), BF16), pltpu.VMEM((nh, 2, s, hw), BF16),
                        pltpu.VMEM((nh, s, 2 * hw), BF16), pltpu.VMEM((nh, 2, TQ, s), F32),
                        pltpu.VMEM((nh, 2 * TQ, s), BF16)],
        compiler_params=_params(2),
        name="diffattn",
    )(lam_p, z, z, z, cos_t, sin_t, ng)


def _mlstm_kernel(q_ref, k_ref, v_ref, o_ref, gcol_ref, grow_ref, cw_ref, cb_ref, brow_ref, bcol_ref,
                  ng_ref, y_ref, pq_ref, pk_ref, c_ref, m_ref):
    @pl.when(pl.program_id(1) == 0)
    def _():
        pq_ref[...] = jnp.zeros_like(pq_ref)
        pk_ref[...] = jnp.zeros_like(pk_ref)
        c_ref[...] = jnp.zeros_like(c_ref)
        m_ref[...] = jnp.zeros_like(m_ref)

    q_raw, k_raw = q_ref[0], k_ref[0]
    y_ref[0] = _mlstm_chunk(q_raw, k_raw, pq_ref[...], pk_ref[...], v_ref[0], o_ref[0], gcol_ref[0],
                            grow_ref[...], cw_ref, cb_ref, brow_ref, bcol_ref, ng_ref, c_ref, m_ref)
    pq_ref[...] = q_raw[L_ML - SUBLANES:]
    pk_ref[...] = k_raw[L_ML - SUBLANES:]


def _mlstm_chunk(q_raw, k_raw, pq, pk, vv, o_pre, gcol, grow, cw_ref, cb_ref, brow_ref, bcol_ref, ng_ref,
                 c_ref, m_ref):
    w = ML_WIDTH
    dh = ML_HEAD_DIM
    ln = q_raw.shape[0]
    qs = _silu(_causal_conv(q_raw, pq, cw_ref[0, :, :w], cb_ref[0, :, :w]))
    ks = _silu(_causal_conv(k_raw, pk, cw_ref[0, :, w:], cb_ref[0, :, w:])) * (dh ** -0.5)
    og = _sigmoid(o_pre)

    gc = gcol + brow_ref[...]
    gr = grow + bcol_ref[...]
    r_i = lax.broadcasted_iota(jnp.int32, (ln, ln), 0)
    c_i = lax.broadcasted_iota(jnp.int32, (ln, ln), 1)
    causal = r_i >= c_i
    b_cols = jnp.dot(causal.astype(F32), jax.nn.log_sigmoid(gc),
                     precision=lax.Precision.HIGHEST, preferred_element_type=F32)
    b_rows = jnp.dot(jax.nn.log_sigmoid(gr), (r_i <= c_i).astype(F32),
                     precision=lax.Precision.HIGHEST, preferred_element_type=F32)

    lane = lax.broadcasted_iota(jnp.int32, (1, LANES), 1)
    sub = lax.broadcasted_iota(jnp.int32, (SUBLANES, 1), 0)
    t_col = lax.broadcasted_iota(jnp.int32, (ln, 1), 0)

    def lane_pick(a, j):
        return jnp.sum(jnp.where(lane == j, a, 0.0), axis=-1, keepdims=True)

    def row_pick(a, j):
        return jnp.sum(jnp.where(sub == j, a, 0.0), axis=0, keepdims=True)

    outs = []
    for hh in range(ML_HEADS):
        sl = slice(hh * dh, (hh + 1) * dh)
        b_col = lane_pick(b_cols, ML_HEADS + hh)
        li_col = lane_pick(gc, hh)
        r_row = row_pick(gr, hh) - row_pick(b_rows, ML_HEADS + hh)
        m_prev = m_ref[hh]
        dmat = jnp.where(causal, b_col + r_row, -jnp.inf)
        m_inter = b_col + m_prev
        m_t = jnp.maximum(m_inter, jnp.max(dmat, axis=-1, keepdims=True))
        qh = qs[:, sl]
        kh = ks[:, sl]
        qb = qh.astype(BF16)
        vb = jnp.concatenate([vv[:, sl].astype(BF16), jnp.ones((ln, dh), BF16)], axis=-1)
        w_intra = lax.dot_general(qb, kh.astype(BF16), (((1,), (1,)), ((), ())),
                                  preferred_element_type=F32) * jnp.exp(dmat - m_t)
        inter = jnp.exp(m_inter - m_t)
        c_prev = c_ref[hh]
        num_den = (inter * jnp.dot(qb, c_prev.astype(BF16), preferred_element_type=F32)
                   + jnp.dot(w_intra.astype(BF16), vb, preferred_element_type=F32))
        hout = num_den[:, :dh] / jnp.maximum(jnp.abs(num_den[:, dh:]), jnp.exp(-m_t))
        outs.append(_rms(hout, ng_ref[0, :, sl]))
        b_last = jnp.sum(jnp.where(t_col == ln - 1, b_col, 0.0), axis=0, keepdims=True)
        g_col = b_last - b_col + li_col
        m_next = jnp.maximum(b_last + m_prev, jnp.max(g_col, axis=0, keepdims=True))
        decay = jnp.exp(b_last + m_prev - m_next)
        kw = kh * jnp.exp(g_col - m_next)
        c_ref[hh] = decay * c_prev + lax.dot_general(kw.astype(BF16), vb, (((0,), (0,)), ((), ())),
                                                     preferred_element_type=F32)
        m_ref[hh] = m_next
    return (jnp.concatenate(outs, axis=-1) * og).astype(BF16)


def _mlstm(z, zg, zgt, cw, cb, brow, bcol, ng, l):
    b, s, _ = z.shape
    w = ML_WIDTH
    nc = s // L_ML
    c0 = ML_Q0 // w
    col = lambda c: pl.BlockSpec((1, L_ML, w), lambda bi, ci: (bi, ci, c))
    return pl.pallas_call(
        _mlstm_kernel,
        grid=(b, nc),
        in_specs=[col(c0), col(c0 + 1), col(c0 + 2), col(c0 + 3),
                  pl.BlockSpec((1, L_ML, LANES), lambda bi, ci: (bi, ci, 0)),
                  pl.BlockSpec((SUBLANES, L_ML), lambda bi, ci: (0, bi * nc + ci)),
                  _layer(cw, l), _layer(cb, l), _full((1, LANES)), _full((SUBLANES, 1)), _layer(ng, l)],
        out_specs=col(0),
        out_shape=jax.ShapeDtypeStruct((b, s, w), BF16),
        scratch_shapes=[pltpu.VMEM((SUBLANES, w), F32), pltpu.VMEM((SUBLANES, w), F32),
                        pltpu.VMEM((ML_HEADS, ML_HEAD_DIM, 2 * ML_HEAD_DIM), F32),
                        pltpu.VMEM((ML_HEADS, 1, 1), F32)],
        compiler_params=_params(2),
        name="mlstm",
    )(z, z, z, z, zg, zgt, cw, cb, brow, bcol, ng)


def _outproj_kernel(x_ref, yr_ref, ya_ref, ym_ref, w_ref, g_ref, x1_ref, h_ref):
    half = TM_OUT // 2
    for r0 in (0, half):
        rows = slice(r0, r0 + half)
        acc = x_ref[rows, :]
        acc = acc + jnp.dot(yr_ref[rows, :], w_ref[0, 0:RG_WIDTH, :], preferred_element_type=F32)
        acc = acc + jnp.dot(ya_ref[rows, :], w_ref[0, RG_WIDTH:RG_WIDTH + DA_WIDTH, :], preferred_element_type=F32)
        acc = acc + jnp.dot(ym_ref[rows, :], w_ref[0, RG_WIDTH + DA_WIDTH:, :], preferred_element_type=F32)
        x1_ref[rows, :] = acc
        h_ref[rows, :] = _rms(acc, g_ref[0]).astype(BF16)


def _outproj(x2d, y_rg, y_da, y_ml, w_out, g, l):
    t, d = x2d.shape
    row = lambda wd: pl.BlockSpec((TM_OUT, wd), lambda i: (i, 0))
    return pl.pallas_call(
        _outproj_kernel,
        grid=(t // TM_OUT,),
        in_specs=[row(d), row(RG_WIDTH), row(DA_WIDTH), row(ML_WIDTH), _layer(w_out, 0), _layer(g, l)],
        out_specs=[row(d), row(d)],
        out_shape=[jax.ShapeDtypeStruct((t, d), F32), jax.ShapeDtypeStruct((t, d), BF16)],
        compiler_params=_params(1),
        name="outproj",
    )(x2d, y_rg, y_da, y_ml, w_out, g)


def _mlp_kernel(h_ref, x_ref, wup_ref, cw_ref, cb_ref, wdn_ref, ng_ref, o_ref, tail_ref, u_ref, act_ref,
                hs_ref, *, final_norm):
    @pl.when(pl.program_id(1) == 0)
    def _():
        tail_ref[...] = jnp.zeros_like(tail_ref)

    tm = h_ref.shape[1]
    hs_ref[...] = h_ref[0]
    n_ch = D_FF // CH_FF
    assert n_ch % 2 == 1 and n_ch * CH_FF == D_FF

    def cols(half, j):
        return pl.ds(half * D_FF + j * CH_FF, CH_FF)

    def up(j, slot):
        for half in range(2):
            u_ref[slot, half] = jnp.dot(hs_ref[...], wup_ref[0, :, cols(half, j)], preferred_element_type=F32)

    def gate(j, slot):
        out_cols = pl.ds(j * CH_FF, CH_FF)
        taps = [(cw_ref[0, :, cols(half, j)] * sc, cb_ref[0, :, cols(half, j)] * sc) for half, sc in ((0, 0.5), (1, 1.0))]
        for r0 in range(0, tm, RB_FF):
            halves = []
            for half in range(2):
                prev = tail_ref[:, cols(half, j)] if r0 == 0 else u_ref[slot, half, r0 - SUBLANES:r0, :]
                halves.append(_causal_conv(u_ref[slot, half, r0:r0 + RB_FF, :], prev, *taps[half]))
            g_half = halves[0]
            act_ref[r0:r0 + RB_FF, out_cols] = (g_half * (1.0 + jnp.tanh(g_half)) * halves[1]).astype(BF16)
        for half in range(2):
            tail_ref[:, cols(half, j)] = u_ref[slot, half, tm - SUBLANES:, :]

    up(0, 0)
    for j in range(n_ch):
        if j + 1 < n_ch:
            up(j + 1, (j + 1) % 2)
        gate(j, j % 2)

    y = x_ref[0] + jnp.dot(act_ref[...], wdn_ref[0], preferred_element_type=F32)
    if final_norm:
        y = _rms(y, ng_ref[...])
    o_ref[0] = y


def _mlp(h, x1, w_up_b, cw, cb, w_down_b, ng, final_norm, l):
    b, s, d = x1.shape
    tile = pl.BlockSpec((1, TM_MLP, d), lambda bi, ti: (bi, ti, 0))
    return pl.pallas_call(
        functools.partial(_mlp_kernel, final_norm=final_norm),
        grid=(b, s // TM_MLP),
        in_specs=[tile, tile, _layer(w_up_b, 0), _layer(cw, l), _layer(cb, l), _layer(w_down_b, 0), _full((1, d))],
        out_specs=tile,
        out_shape=jax.ShapeDtypeStruct((b, s, d), F32),
        scratch_shapes=[pltpu.VMEM((SUBLANES, 2 * D_FF), F32), pltpu.VMEM((2, 2, TM_MLP, CH_FF), F32),
                        pltpu.VMEM((TM_MLP, D_FF), BF16), pltpu.VMEM((TM_MLP, d), BF16)],
        compiler_params=_params(2),
        name="mlp",
    )(h, x1, w_up_b, cw, cb, w_down_b, ng)


def kernel(x, positions, attn_norm, w_in, rg_conv_w, rg_conv_b, rg_wa, rg_ba, rg_wx, rg_bx, rg_lambda, rg_norm, da_lambda, da_norm, ml_conv_w, ml_conv_b, ml_i_bias, ml_f_bias, ml_norm, w_out, mlp_norm, w_up, ffn_conv_w, ffn_conv_b, w_down, final_norm):
    b, s, d = x.shape
    depth = w_in.shape[0]
    t = b * s
    n_gate = 2 * ML_HEADS
    cos_t, sin_t = _rope_tables(positions)
    w_in_b, wa_b, wx_b = w_in.astype(BF16), rg_wa.astype(BF16), rg_wx.astype(BF16)
    w_gate_b = jnp.pad(w_in[:, :, D_MAIN:], ((0, 0), (0, 0), (0, LANES - n_gate))).astype(BF16)
    gate_bias = jnp.concatenate([ml_i_bias, ml_f_bias], axis=-1)
    attn_norm3, mlp_norm3, rg_norm3, da_norm3, ml_norm3 = map(_rows3, (attn_norm, mlp_norm, rg_norm, da_norm, ml_norm))
    rg_cb3, rg_ba3, rg_bx3, rg_lam3, ml_cb3, ffn_cb3 = map(_rows3, (rg_conv_b, rg_ba, rg_bx, rg_lambda, ml_conv_b, ffn_conv_b))
    for l in range(depth):
        lambda_init = 0.8 - 0.6 * math.exp(-0.3 * l)
        z, zg, zgt, w_up_b, w_down_b, w_out_b = _inproj(x.reshape(t, d), attn_norm3, w_in_b, w_gate_b[l],
                                                        w_up, w_down, w_out, l)
        z = z.reshape(b, s, D_MAIN)
        zg = zg.reshape(b, s, LANES)
        y_rg = _rglru(z, rg_conv_w, rg_cb3, wa_b, rg_ba3, wx_b, rg_bx3, rg_lam3, rg_norm3, l)
        y_da = _attention(z, cos_t, sin_t, da_lambda, da_norm3, lambda_init, l)
        y_ml = _mlstm(z, zg, zgt, ml_conv_w, ml_cb3,
                      jnp.pad(gate_bias[l], (0, LANES - n_gate)).reshape(1, LANES),
                      gate_bias[l].reshape(n_gate, 1), ml_norm3, l)
        x1, h2 = _outproj(x.reshape(t, d), y_rg.reshape(t, -1), y_da.reshape(t, -1), y_ml.reshape(t, -1),
                          w_out_b, mlp_norm3, l)
        x = _mlp(h2.reshape(b, s, d), x1.reshape(b, s, d), w_up_b, ffn_conv_w, ffn_cb3, w_down_b,
                 final_norm.reshape(1, d), l == depth - 1, l)
    return x
```

```python
import functools
import math

import jax
import jax.numpy as jnp
from jax import lax
from jax.experimental import pallas as pl
from jax.experimental.pallas import tpu as pltpu

F32 = jnp.float32
BF16 = jnp.bfloat16

D_MODEL = 1024
RG_WIDTH = 512
RG_BLOCKS = 4
RG_BLOCK = RG_WIDTH // RG_BLOCKS
RG_CONV = 4
RG_C = 8.0
DA_HEADS = 4
DA_HEAD_DIM = 64
DA_WIDTH = DA_HEADS * 2 * DA_HEAD_DIM
ROPE_THETA = 500000.0
ROPE_DIM = DA_HEAD_DIM // 4
NEG_INF = -1e30
ML_HEADS = 4
ML_HEAD_DIM = 128
ML_WIDTH = ML_HEADS * ML_HEAD_DIM
ML_CONV = 4
D_MIX = RG_WIDTH + DA_WIDTH + ML_WIDTH
D_FF = 2816
FFN_CONV = 3
EPS = 1e-6
IN_WIDTHS = (RG_WIDTH, RG_WIDTH, DA_WIDTH, DA_WIDTH, DA_WIDTH,
             ML_WIDTH, ML_WIDTH, ML_WIDTH, ML_WIDTH, ML_HEADS, ML_HEADS)
D_IN = sum(IN_WIDTHS)

LANES = 128
SUBLANES = 8
BF16_SUBLANES = 16
D_MAIN = D_IN - 2 * ML_HEADS
ML_Q0 = 2 * RG_WIDTH + 3 * DA_WIDTH
VMEM_LIMIT = 56 * 1024 * 1024

TM_IN = 512
TN_IN = 1536
TM_OUT = 1024
TT_RG = 1024
SCAN_BLOCK = SUBLANES * SUBLANES
TQ = 256
TK = 256
HEADS_PER_STEP = 2
L_ML = 256
TM_MLP = 1024
CH_FF = 256
RB_FF = 128


def _params(n_axes):
    return pltpu.CompilerParams(dimension_semantics=("arbitrary",) * n_axes,
                                vmem_limit_bytes=VMEM_LIMIT)


def _full(shape):
    nd = len(shape)
    return pl.BlockSpec(shape, lambda *_: (0,) * nd)


def _layer(arr, l):
    nd = arr.ndim
    return pl.BlockSpec((1,) + arr.shape[1:], lambda *_: (l,) + (0,) * (nd - 1),
                        pipeline_mode=pl.Buffered(1))


def _cast_specs(w, l, steps):
    _, r, c = w.shape
    k = 1
    while steps % k or r % (steps // k) or (r // (steps // k)) % BF16_SUBLANES:
        k *= 2
        assert k <= steps, (r, steps)
    rows = r // (steps // k)
    in_spec = pl.BlockSpec((1, rows, c), lambda i: (l, i // k, 0))
    out_spec = pl.BlockSpec((1, rows, c), lambda i: (0, i // k, 0))
    return in_spec, out_spec, jax.ShapeDtypeStruct((1, r, c), BF16)


def _rows3(arr):
    return arr.reshape(arr.shape[0], 1, arr.shape[1])


def _shift_rows(x, prev8, d):
    rolled = pltpu.roll(x, d, axis=0)
    prolled = pltpu.roll(prev8, d, axis=0)
    row = lax.broadcasted_iota(jnp.int32, prev8.shape, 0)
    top = jnp.where(row < d, prolled, rolled[:SUBLANES])
    return jnp.concatenate([top, rolled[SUBLANES:]], axis=0)


def _causal_conv(x, prev8, w, b):
    k = w.shape[0]
    y = x * w[k - 1:k]
    for d in range(1, k):
        y = y + _shift_rows(x, prev8, d) * w[k - 1 - d:k - d]
    return y + b


def _rms(x, g):
    return x * lax.rsqrt(jnp.mean(x * x, axis=-1, keepdims=True) + EPS) * g


def _sigmoid(x):
    return 0.5 + 0.5 * jnp.tanh(0.5 * x)


def _silu(x):
    h = 0.5 * x
    return h * (1.0 + jnp.tanh(h))


def _inproj_kernel(x_ref, xn_ref, g_ref, w_ref, wg_ref, wu_ref, wd_ref, wo_ref,
                   z_ref, zg_ref, zgt_ref, wub_ref, wdb_ref, wob_ref, ha_ref, hb_ref):
    half = TM_IN // 2
    g = g_ref[0]
    wub_ref[...] = wu_ref[...].astype(BF16)
    wdb_ref[...] = wd_ref[...].astype(BF16)
    wob_ref[...] = wo_ref[...].astype(BF16)

    @pl.when(pl.program_id(0) == 0)
    def _():
        ha_ref[...] = _rms(x_ref[:half], g).astype(BF16)

    def project(h, rows):
        for c0 in range(0, D_MAIN, TN_IN):
            z_ref[rows, c0:c0 + TN_IN] = jnp.dot(h, w_ref[0, :, c0:c0 + TN_IN], preferred_element_type=F32)
        zg = jnp.dot(h, wg_ref[...], preferred_element_type=F32)
        zg_ref[rows, :] = zg
        zgt_ref[:, rows] = zg.T[:SUBLANES]

    hb_ref[...] = _rms(x_ref[half:], g).astype(BF16)
    project(ha_ref[...], slice(0, half))
    ha_next = _rms(xn_ref[...], g).astype(BF16)
    project(hb_ref[...], slice(half, TM_IN))
    ha_ref[...] = ha_next


def _inproj(x2d, g, w_in_b, w_gate, w_up, w_down, w_out, l):
    t, d = x2d.shape
    half = TM_IN // 2
    last_half = t // half - 1
    steps = t // TM_IN
    casts = [_cast_specs(w, l, steps) for w in (w_up, w_down, w_out)]
    return pl.pallas_call(
        _inproj_kernel,
        grid=(steps,),
        in_specs=[pl.BlockSpec((TM_IN, d), lambda i: (i, 0)),
                  pl.BlockSpec((half, d), lambda i: (jnp.minimum(2 * i + 2, last_half), 0)),
                  _layer(g, l), _layer(w_in_b, l), _full((d, LANES))] + [c[0] for c in casts],
        out_specs=[pl.BlockSpec((TM_IN, D_MAIN), lambda i: (i, 0)),
                   pl.BlockSpec((TM_IN, LANES), lambda i: (i, 0)),
                   pl.BlockSpec((SUBLANES, TM_IN), lambda i: (0, i))] + [c[1] for c in casts],
        out_shape=[jax.ShapeDtypeStruct((t, D_MAIN), F32), jax.ShapeDtypeStruct((t, LANES), F32),
                   jax.ShapeDtypeStruct((SUBLANES, t), F32)] + [c[2] for c in casts],
        scratch_shapes=[pltpu.VMEM((half, d), BF16), pltpu.VMEM((half, d), BF16)],
        compiler_params=_params(1),
        name="inproj",
    )(x2d, x2d, g, w_in_b, w_gate, w_up, w_down, w_out)


def _rglru_tile(x, gate, cw_ref, cb_ref, wa_ref, ba_ref, wx_ref, bx_ref, lam_ref, ng_ref,
                prev_ref, carry_ref, a_ref, b_ref, h_ref):
    tt = x.shape[0]
    n_slabs = x.shape[1] // LANES
    u = _causal_conv(x, prev_ref[...], cw_ref[0], cb_ref[0])
    prev_ref[...] = x[tt - SUBLANES:]
    ub = u.astype(BF16)
    ra, ri = [], []
    for n in range(RG_BLOCKS):
        un = ub[:, n * RG_BLOCK:(n + 1) * RG_BLOCK]
        ra.append(jnp.dot(un, wa_ref[0, n], preferred_element_type=F32))
        ri.append(jnp.dot(un, wx_ref[0, n], preferred_element_type=F32))
    r = _sigmoid(jnp.concatenate(ra, axis=-1) + ba_ref[0])
    i = _sigmoid(jnp.concatenate(ri, axis=-1) + bx_ref[0])
    log_a = r * (RG_C * jax.nn.log_sigmoid(lam_ref[0]))
    a = jnp.exp(log_a)
    gain2 = -jnp.tanh(log_a) * (a * a + 1.0)
    gain = jnp.where(gain2 > 0.0, gain2 * lax.rsqrt(gain2), 0.0)
    bt = gain * (i * u)
    for s in range(n_slabs):
        a_ref[s] = a[:, s * LANES:(s + 1) * LANES]
        b_ref[s] = bt[:, s * LANES:(s + 1) * LANES]

    sub = lax.broadcasted_iota(jnp.int32, (SUBLANES, LANES), 0)
    for s in range(n_slabs):
        lanes = slice(s * LANES, (s + 1) * LANES)
        carry = jnp.broadcast_to(carry_ref[:, lanes], (SUBLANES, LANES))
        for base in range(0, tt, SCAN_BLOCK):
            acc_a, acc_h = [], []
            for k in range(SUBLANES):
                rows = pl.ds(base + k, SUBLANES, stride=SUBLANES)
                ak, bk = a_ref[s, rows, :], b_ref[s, rows, :]
                acc_h.append(bk if k == 0 else ak * acc_h[-1] + bk)
                acc_a.append(ak if k == 0 else ak * acc_a[-1])
            seg_a, seg_h = acc_a[-1], acc_h[-1]
            for d in (1, 2, 4):
                keep = sub >= d
                seg_h = jnp.where(keep, seg_a * pltpu.roll(seg_h, d, axis=0) + seg_h, seg_h)
                seg_a = jnp.where(keep, seg_a * pltpu.roll(seg_a, d, axis=0), seg_a)
            seg_end = seg_a * carry + seg_h
            enter = jnp.where(sub == 0, carry, pltpu.roll(seg_end, 1, axis=0))
            for k in range(SUBLANES):
                h_ref[s, pl.ds(base + k, SUBLANES, stride=SUBLANES), :] = acc_h[k] + acc_a[k] * enter
            carry = jnp.broadcast_to(seg_end[SUBLANES - 1:, :], (SUBLANES, LANES))
        carry_ref[:, lanes] = carry[:1]

    h = jnp.concatenate([h_ref[s] for s in range(n_slabs)], axis=-1)
    k1 = math.sqrt(2.0 / math.pi)
    y = (0.5 * gate) * (1.0 + jnp.tanh(gate * (k1 + (k1 * 0.044715) * (gate * gate)))) * h
    return _rms(y, ng_ref[0]).astype(BF16)


def _rglru_kernel(x_ref, gate_ref, cw_ref, cb_ref, wa_ref, ba_ref, wx_ref, bx_ref, lam_ref, ng_ref,
                  y_ref, prev_ref, carry_ref, a_ref, b_ref, h_ref):
    @pl.when(pl.program_id(1) == 0)
    def _():
        prev_ref[...] = jnp.zeros_like(prev_ref)
        carry_ref[...] = jnp.zeros_like(carry_ref)

    y_ref[0] = _rglru_tile(x_ref[0], gate_ref[0], cw_ref, cb_ref, wa_ref, ba_ref, wx_ref, bx_ref, lam_ref, ng_ref,
                           prev_ref, carry_ref, a_ref, b_ref, h_ref)


def _rglru(z, cw, cb, wa, ba, wx, bx, lam, ng, l):
    b, s, _ = z.shape
    w = RG_WIDTH
    slabs = pltpu.VMEM((w // LANES, TT_RG, LANES), F32)
    return pl.pallas_call(
        _rglru_kernel,
        grid=(b, s // TT_RG),
        in_specs=[pl.BlockSpec((1, TT_RG, w), lambda bi, ti: (bi, ti, 0)),
                  pl.BlockSpec((1, TT_RG, w), lambda bi, ti: (bi, ti, 1)),
                  _layer(cw, l), _layer(cb, l), _layer(wa, l), _layer(ba, l), _layer(wx, l), _layer(bx, l),
                  _layer(lam, l), _layer(ng, l)],
        out_specs=pl.BlockSpec((1, TT_RG, w), lambda bi, ti: (bi, ti, 0)),
        out_shape=jax.ShapeDtypeStruct((b, s, w), BF16),
        scratch_shapes=[pltpu.VMEM((SUBLANES, w), F32), pltpu.VMEM((1, w), F32), slabs, slabs, slabs],
        compiler_params=_params(2),
        name="rglru",
    )(z, z, cw, cb, wa, ba, wx, bx, lam, ng)


def _rope_table_kernel(pos_ref, f_ref, c_ref, s_ref):
    ang = pos_ref[0].astype(F32) * f_ref[...]
    cos, sin = jnp.cos(ang), jnp.sin(ang)
    rest = (DA_HEAD_DIM - ROPE_DIM, ang.shape[1])
    c_map = jnp.concatenate([cos, cos, jnp.ones(rest, F32)], axis=0)
    s_map = jnp.concatenate([-sin, sin, jnp.zeros(rest, F32)], axis=0)
    c_ref[0] = jnp.concatenate([c_map, c_map], axis=0).T
    s_ref[0] = jnp.concatenate([s_map, s_map], axis=0).T


def _rope_tables(positions):
    b, s = positions.shape
    half = ROPE_DIM // 2
    inv_freq = ROPE_THETA ** (-jnp.arange(0, ROPE_DIM, 2, dtype=F32) / ROPE_DIM)
    tab = pl.BlockSpec((1, s, 2 * DA_HEAD_DIM), lambda bi: (bi, 0, 0))
    return pl.pallas_call(
        _rope_table_kernel,
        grid=(b,),
        in_specs=[pl.BlockSpec((1, 1, s), lambda bi: (bi, 0, 0)), _full((half, 1))],
        out_specs=[tab, tab],
        out_shape=[jax.ShapeDtypeStruct((b, s, 2 * DA_HEAD_DIM), F32)] * 2,
        compiler_params=_params(1),
        name="rope_tables",
    )(positions.reshape(b, 1, s), inv_freq.reshape(half, 1))


def _rope(x, c, s):
    half = ROPE_DIM // 2
    lane = lax.broadcasted_iota(jnp.int32, (1, LANES), 1)
    first = (lane % DA_HEAD_DIM) < half
    partner = jnp.where(first, pltpu.roll(x, LANES - half, axis=1), pltpu.roll(x, half, axis=1))
    return x * c + partner * s


def _attn_kernel(lam_ref, q_ref, k_ref, v_ref, cos_ref, sin_ref, ng_ref, o_ref,
                 qb_ref, kz_ref, vb_ref, sc_ref, p_ref, *, lambda_init):
    hw = 2 * DA_HEAD_DIM
    for hh in range(q_ref.shape[2] // hw):
        lanes = pl.ds(hh * hw, hw)
        _attn_head(lam_ref, q_ref.at[:, :, lanes], k_ref.at[:, :, lanes], v_ref.at[:, :, lanes], cos_ref, sin_ref,
                   ng_ref, o_ref.at[:, :, lanes], qb_ref.at[hh], kz_ref.at[hh], vb_ref.at[hh], sc_ref.at[hh],
                   p_ref.at[hh], lambda_init=lambda_init)


def _attn_head(lam_ref, q_ref, k_ref, v_ref, cos_ref, sin_ref, ng_ref, o_ref,
               qb_ref, kz_ref, vb_ref, sc_ref, p_ref, *, lambda_init):
    s_len = k_ref.shape[1]
    nt = s_len // TK
    groups = TK // LANES
    lane = lax.broadcasted_iota(jnp.int32, (1, LANES), 1)
    map0 = lane < DA_HEAD_DIM
    q_scale = DA_HEAD_DIM ** -0.5 * math.log2(math.e)

    for j in range(nt):
        rows = slice(j * TK, (j + 1) * TK)
        cos, sin = cos_ref[0, rows, :], sin_ref[0, rows, :]
        kr = _rope(k_ref[0, rows, :], cos, sin)
        kz_ref[0, rows, :] = jnp.where(map0, kr, 0.0).astype(BF16)
        kz_ref[1, rows, :] = jnp.where(map0, 0.0, kr).astype(BF16)
        vb_ref[rows, :] = jnp.concatenate(
            [v_ref[0, rows, :], jnp.broadcast_to(jnp.where(lane == 0, 1.0, 0.0), (TK, LANES))], axis=-1).astype(BF16)
        qb_ref[rows, :] = (_rope(q_ref[0, rows, :], cos, sin) * q_scale).astype(BF16)

    lp = lam_ref[0]
    lam = (jnp.exp(jnp.sum(lp[0:1] * lp[1:2], axis=-1, keepdims=True))
           - jnp.exp(jnp.sum(lp[2:3] * lp[3:4], axis=-1, keepdims=True)) + lambda_init)
    on_or_below_diag = (lax.broadcasted_iota(jnp.int32, (TQ, TK), 0)
                        >= lax.broadcasted_iota(jnp.int32, (TQ, TK), 1))

    for qi in reversed(range(nt)):
        kv = (qi + 1) * TK
        q = qb_ref[qi * TQ:(qi + 1) * TQ, :]
        for c in range(2):
            sc_ref[c, :, :kv] = lax.dot_general(q, kz_ref[c, :kv, :], (((1,), (1,)), ((), ())),
                                                preferred_element_type=F32)
        for c in range(2):
            m = None
            for j in range(qi + 1):
                cols = slice(j * TK, (j + 1) * TK)
                s = sc_ref[c, :, cols]
                if j == qi:
                    s = jnp.where(on_or_below_diag, s, NEG_INF)
                    sc_ref[c, :, cols] = s
                for g in range(groups):
                    sg = s[:, g * LANES:(g + 1) * LANES]
                    m = sg if m is None else jnp.maximum(m, sg)
            m_b = jnp.broadcast_to(jnp.max(m, axis=-1, keepdims=True), (TQ, LANES))
            m_b = jnp.concatenate([m_b] * groups, axis=-1)
            for j in range(qi + 1):
                cols = slice(j * TK, (j + 1) * TK)
                p_ref[c * TQ:(c + 1) * TQ, cols] = jnp.exp2(sc_ref[c, :, cols] - m_b).astype(BF16)
        acc = jnp.dot(p_ref[:, :kv], vb_ref[:kv, :], preferred_element_type=F32)
        hw = 2 * DA_HEAD_DIM
        o = (acc[:TQ, :hw] / acc[:TQ, hw:hw + 1]) - lam * (acc[TQ:, :hw] / acc[TQ:, hw:hw + 1])
        o_ref[0, qi * TQ:(qi + 1) * TQ, :] = (_rms(o, ng_ref[0]) * (1.0 - lambda_init)).astype(BF16)


def _attention(z, cos_t, sin_t, lam_p, ng, lambda_init, l):
    b, s, _ = z.shape
    hw = 2 * DA_HEAD_DIM
    q0 = (2 * RG_WIDTH) // hw
    k0 = q0 + DA_HEADS
    v0 = k0 + DA_HEADS
    nh = HEADS_PER_STEP
    assert q0 % nh == 0 and DA_HEADS % nh == 0
    seq = lambda c0: pl.BlockSpec((1, s, nh * hw), lambda bi, hi: (bi, 0, c0 // nh + hi))
    tab = pl.BlockSpec((1, s, hw), lambda bi, hi: (bi, 0, 0))
    return pl.pallas_call(
        functools.partial(_attn_kernel, lambda_init=lambda_init),
        grid=(b, DA_HEADS // nh),
        in_specs=[_layer(lam_p, l), seq(q0), seq(k0), seq(v0), tab, tab, _layer(ng, l)],
        out_specs=pl.BlockSpec((1, s, nh * hw), lambda bi, hi: (bi, 0, hi)),
        out_shape=jax.ShapeDtypeStruct((b, s, DA_WIDTH), BF16),
        scratch_shapes=[pltpu.VMEM((nh, s, hw), BF16), pltpu.VMEM((nh, 2, s, hw), BF16),
                        pltpu.VMEM((nh, s, 2 * hw), BF16), pltpu.VMEM((nh, 2, TQ, s), F32),
                        pltpu.VMEM((nh, 2 * TQ, s), BF16)],
        compiler_params=_params(2),
        name="diffattn",
    )(lam_p, z, z, z, cos_t, sin_t, ng)


def _mlstm_kernel(q_ref, k_ref, v_ref, o_ref, gcol_ref, grow_ref, cw_ref, cb_ref, brow_ref, bcol_ref,
                  ng_ref, y_ref, pq_ref, pk_ref, c_ref, m_ref):
    @pl.when(pl.program_id(1) == 0)
    def _():
        pq_ref[...] = jnp.zeros_like(pq_ref)
        pk_ref[...] = jnp.zeros_like(pk_ref)
        c_ref[...] = jnp.zeros_like(c_ref)
        m_ref[...] = jnp.zeros_like(m_ref)

    q_raw, k_raw = q_ref[0], k_ref[0]
    y_ref[0] = _mlstm_chunk(q_raw, k_raw, pq_ref[...], pk_ref[...], v_ref[0], o_ref[0], gcol_ref[0],
                            grow_ref[...], cw_ref, cb_ref, brow_ref, bcol_ref, ng_ref, c_ref, m_ref)
    pq_ref[...] = q_raw[L_ML - SUBLANES:]
    pk_ref[...] = k_raw[L_ML - SUBLANES:]


def _mlstm_chunk(q_raw, k_raw, pq, pk, vv, o_pre, gcol, grow, cw_ref, cb_ref, brow_ref, bcol_ref, ng_ref,
                 c_ref, m_ref):
    w = ML_WIDTH
    dh = ML_HEAD_DIM
    ln = q_raw.shape[0]
    qs = _silu(_causal_conv(q_raw, pq, cw_ref[0, :, :w], cb_ref[0, :, :w]))
    ks = _silu(_causal_conv(k_raw, pk, cw_ref[0, :, w:], cb_ref[0, :, w:])) * (dh ** -0.5)
    og = _sigmoid(o_pre)

    gc = gcol + brow_ref[...]
    gr = grow + bcol_ref[...]
    r_i = lax.broadcasted_iota(jnp.int32, (ln, ln), 0)
    c_i = lax.broadcasted_iota(jnp.int32, (ln, ln), 1)
    causal = r_i >= c_i
    b_cols = jnp.dot(causal.astype(F32), jax.nn.log_sigmoid(gc),
                     precision=lax.Precision.HIGHEST, preferred_element_type=F32)
    b_rows = jnp.dot(jax.nn.log_sigmoid(gr), (r_i <= c_i).astype(F32),
                     precision=lax.Precision.HIGHEST, preferred_element_type=F32)

    lane = lax.broadcasted_iota(jnp.int32, (1, LANES), 1)
    sub = lax.broadcasted_iota(jnp.int32, (SUBLANES, 1), 0)
    t_col = lax.broadcasted_iota(jnp.int32, (ln, 1), 0)

    def lane_pick(a, j):
        return jnp.sum(jnp.where(lane == j, a, 0.0), axis=-1, keepdims=True)

    def row_pick(a, j):
        return jnp.sum(jnp.where(sub == j, a, 0.0), axis=0, keepdims=True)

    outs = []
    for hh in range(ML_HEADS):
        sl = slice(hh * dh, (hh + 1) * dh)
        b_col = lane_pick(b_cols, ML_HEADS + hh)
        li_col = lane_pick(gc, hh)
        r_row = row_pick(gr, hh) - row_pick(b_rows, ML_HEADS + hh)
        m_prev = m_ref[hh]
        dmat = jnp.where(causal, b_col + r_row, -jnp.inf)
        m_inter = b_col + m_prev
        m_t = jnp.maximum(m_inter, jnp.max(dmat, axis=-1, keepdims=True))
        qh = qs[:, sl]
        kh = ks[:, sl]
        qb = qh.astype(BF16)
        vb = jnp.concatenate([vv[:, sl].astype(BF16), jnp.ones((ln, dh), BF16)], axis=-1)
        w_intra = lax.dot_general(qb, kh.astype(BF16), (((1,), (1,)), ((), ())),
                                  preferred_element_type=F32) * jnp.exp(dmat - m_t)
        inter = jnp.exp(m_inter - m_t)
        c_prev = c_ref[hh]
        num_den = (inter * jnp.dot(qb, c_prev.astype(BF16), preferred_element_type=F32)
                   + jnp.dot(w_intra.astype(BF16), vb, preferred_element_type=F32))
        hout = num_den[:, :dh] / jnp.maximum(jnp.abs(num_den[:, dh:]), jnp.exp(-m_t))
        outs.append(_rms(hout, ng_ref[0, :, sl]))
        b_last = jnp.sum(jnp.where(t_col == ln - 1, b_col, 0.0), axis=0, keepdims=True)
        g_col = b_last - b_col + li_col
        m_next = jnp.maximum(b_last + m_prev, jnp.max(g_col, axis=0, keepdims=True))
        decay = jnp.exp(b_last + m_prev - m_next)
        kw = kh * jnp.exp(g_col - m_next)
        c_ref[hh] = decay * c_prev + lax.dot_general(kw.astype(BF16), vb, (((0,), (0,)), ((), ())),
                                                     preferred_element_type=F32)
        m_ref[hh] = m_next
    return (jnp.concatenate(outs, axis=-1) * og).astype(BF16)


def _mlstm(z, zg, zgt, cw, cb, brow, bcol, ng, l):
    b, s, _ = z.shape
    w = ML_WIDTH
    nc = s // L_ML
    c0 = ML_Q0 // w
    col = lambda c: pl.BlockSpec((1, L_ML, w), lambda bi, ci: (bi, ci, c))
    return pl.pallas_call(
        _mlstm_kernel,
        grid=(b, nc),
        in_specs=[col(c0), col(c0 + 1), col(c0 + 2), col(c0 + 3),
                  pl.BlockSpec((1, L_ML, LANES), lambda bi, ci: (bi, ci, 0)),
                  pl.BlockSpec((SUBLANES, L_ML), lambda bi, ci: (0, bi * nc + ci)),
                  _layer(cw, l), _layer(cb, l), _full((1, LANES)), _full((SUBLANES, 1)), _layer(ng, l)],
        out_specs=col(0),
        out_shape=jax.ShapeDtypeStruct((b, s, w), BF16),
        scratch_shapes=[pltpu.VMEM((SUBLANES, w), F32), pltpu.VMEM((SUBLANES, w), F32),
                        pltpu.VMEM((ML_HEADS, ML_HEAD_DIM, 2 * ML_HEAD_DIM), F32),
                        pltpu.VMEM((ML_HEADS, 1, 1), F32)],
        compiler_params=_params(2),
        name="mlstm",
    )(z, z, z, z, zg, zgt, cw, cb, brow, bcol, ng)


def _outproj_kernel(x_ref, yr_ref, ya_ref, ym_ref, w_ref, g_ref, x1_ref, h_ref):
    half = TM_OUT // 2
    for r0 in (0, half):
        rows = slice(r0, r0 + half)
        acc = x_ref[rows, :]
        acc = acc + jnp.dot(yr_ref[rows, :], w_ref[0, 0:RG_WIDTH, :], preferred_element_type=F32)
        acc = acc + jnp.dot(ya_ref[rows, :], w_ref[0, RG_WIDTH:RG_WIDTH + DA_WIDTH, :], preferred_element_type=F32)
        acc = acc + jnp.dot(ym_ref[rows, :], w_ref[0, RG_WIDTH + DA_WIDTH:, :], preferred_element_type=F32)
        x1_ref[rows, :] = acc
        h_ref[rows, :] = _rms(acc, g_ref[0]).astype(BF16)


def _outproj(x2d, y_rg, y_da, y_ml, w_out, g, l):
    t, d = x2d.shape
    row = lambda wd: pl.BlockSpec((TM_OUT, wd), lambda i: (i, 0))
    return pl.pallas_call(
        _outproj_kernel,
        grid=(t // TM_OUT,),
        in_specs=[row(d), row(RG_WIDTH), row(DA_WIDTH), row(ML_WIDTH), _layer(w_out, 0), _layer(g, l)],
        out_specs=[row(d), row(d)],
        out_shape=[jax.ShapeDtypeStruct((t, d), F32), jax.ShapeDtypeStruct((t, d), BF16)],
        compiler_params=_params(1),
        name="outproj",
    )(x2d, y_rg, y_da, y_ml, w_out, g)


def _mlp_kernel(h_ref, x_ref, wup_ref, cw_ref, cb_ref, wdn_ref, ng_ref, o_ref, tail_ref, u_ref, act_ref,
                hs_ref, *, final_norm):
    @pl.when(pl.program_id(1) == 0)
    def _():
        tail_ref[...] = jnp.zeros_like(tail_ref)

    tm = h_ref.shape[1]
    hs_ref[...] = h_ref[0]
    n_ch = D_FF // CH_FF
    assert n_ch % 2 == 1 and n_ch * CH_FF == D_FF

    def cols(half, j):
        return pl.ds(half * D_FF + j * CH_FF, CH_FF)

    def up(j, slot):
        for half in range(2):
            u_ref[slot, half] = jnp.dot(hs_ref[...], wup_ref[0, :, cols(half, j)], preferred_element_type=F32)

    def gate(j, slot):
        out_cols = pl.ds(j * CH_FF, CH_FF)
        taps = [(cw_ref[0, :, cols(half, j)] * sc, cb_ref[0, :, cols(half, j)] * sc) for half, sc in ((0, 0.5), (1, 1.0))]
        for r0 in range(0, tm, RB_FF):
            halves = []
            for half in range(2):
                prev = tail_ref[:, cols(half, j)] if r0 == 0 else u_ref[slot, half, r0 - SUBLANES:r0, :]
                halves.append(_causal_conv(u_ref[slot, half, r0:r0 + RB_FF, :], prev, *taps[half]))
            g_half = halves[0]
            act_ref[r0:r0 + RB_FF, out_cols] = (g_half * (1.0 + jnp.tanh(g_half)) * halves[1]).astype(BF16)
        for half in range(2):
            tail_ref[:, cols(half, j)] = u_ref[slot, half, tm - SUBLANES:, :]

    up(0, 0)
    for j in range(n_ch):
        if j + 1 < n_ch:
            up(j + 1, (j + 1) % 2)
        gate(j, j % 2)

    y = x_ref[0] + jnp.dot(act_ref[...], wdn_ref[0], preferred_element_type=F32)
    if final_norm:
        y = _rms(y, ng_ref[...])
    o_ref[0] = y


def _mlp(h, x1, w_up_b, cw, cb, w_down_b, ng, final_norm, l):
    b, s, d = x1.shape
    tile = pl.BlockSpec((1, TM_MLP, d), lambda bi, ti: (bi, ti, 0))
    return pl.pallas_call(
        functools.partial(_mlp_kernel, final_norm=final_norm),
        grid=(b, s // TM_MLP),
        in_specs=[tile, tile, _layer(w_up_b, 0), _layer(cw, l), _layer(cb, l), _layer(w_down_b, 0), _full((1, d))],
        out_specs=tile,
        out_shape=jax.ShapeDtypeStruct((b, s, d), F32),
        scratch_shapes=[pltpu.VMEM((SUBLANES, 2 * D_FF), F32), pltpu.VMEM((2, 2, TM_MLP, CH_FF), F32),
                        pltpu.VMEM((TM_MLP, D_FF), BF16), pltpu.VMEM((TM_MLP, d), BF16)],
        compiler_params=_params(2),
        name="mlp",
    )(h, x1, w_up_b, cw, cb, w_down_b, ng)


def kernel(x, positions, attn_norm, w_in, rg_conv_w, rg_conv_b, rg_wa, rg_ba, rg_wx, rg_bx, rg_lambda, rg_norm, da_lambda, da_norm, ml_conv_w, ml_conv_b, ml_i_bias, ml_f_bias, ml_norm, w_out, mlp_norm, w_up, ffn_conv_w, ffn_conv_b, w_down, final_norm):
    b, s, d = x.shape
    depth = w_in.shape[0]
    t = b * s
    n_gate = 2 * ML_HEADS
    cos_t, sin_t = _rope_tables(positions)
    w_in_b, wa_b, wx_b = w_in.astype(BF16), rg_wa.astype(BF16), rg_wx.astype(BF16)
    w_gate_b = jnp.pad(w_in[:, :, D_MAIN:], ((0, 0), (0, 0), (0, LANES - n_gate))).astype(BF16)
    gate_bias = jnp.concatenate([ml_i_bias, ml_f_bias], axis=-1)
    attn_norm3, mlp_norm3, rg_norm3, da_norm3, ml_norm3 = map(_rows3, (attn_norm, mlp_norm, rg_norm, da_norm, ml_norm))
    rg_cb3, rg_ba3, rg_bx3, rg_lam3, ml_cb3, ffn_cb3 = map(_rows3, (rg_conv_b, rg_ba, rg_bx, rg_lambda, ml_conv_b, ffn_conv_b))
    for l in range(depth):
        lambda_init = 0.8 - 0.6 * math.exp(-0.3 * l)
        z, zg, zgt, w_up_b, w_down_b, w_out_b = _inproj(x.reshape(t, d), attn_norm3, w_in_b, w_gate_b[l],
                                                        w_up, w_down, w_out, l)
        z = z.reshape(b, s, D_MAIN)
        zg = zg.reshape(b, s, LANES)
        y_rg = _rglru(z, rg_conv_w, rg_cb3, wa_b, rg_ba3, wx_b, rg_bx3, rg_lam3, rg_norm3, l)
        y_da = _attention(z, cos_t, sin_t, da_lambda, da_norm3, lambda_init, l)
        y_ml = _mlstm(z, zg, zgt, ml_conv_w, ml_cb3,
                      jnp.pad(gate_bias[l], (0, LANES - n_gate)).reshape(1, LANES),
                      gate_bias[l].reshape(n_gate, 1), ml_norm3, l)
        x1, h2 = _outproj(x.reshape(t, d), y_rg.reshape(t, -1), y_da.reshape(t, -1), y_ml.reshape(t, -1),
                          w_out_b, mlp_norm3, l)
        x = _mlp(h2.reshape(b, s, d), x1.reshape(b, s, d), w_up_b, ffn_conv_w, ffn_cb3, w_down_b,
                 final_norm.reshape(1, d), l == depth - 1, l)
    return x
```

```python
import functools
import math

import jax
import jax.numpy as jnp
from jax import lax
from jax.experimental import pallas as pl
from jax.experimental.pallas import tpu as pltpu

F32 = jnp.float32
BF16 = jnp.bfloat16

D_MODEL = 1024
RG_WIDTH = 512
RG_BLOCKS = 4
RG_BLOCK = RG_WIDTH // RG_BLOCKS
RG_CONV = 4
RG_C = 8.0
DA_HEADS = 4
DA_HEAD_DIM = 64
DA_WIDTH = DA_HEADS * 2 * DA_HEAD_DIM
ROPE_THETA = 500000.0
ROPE_DIM = DA_HEAD_DIM // 4
NEG_INF = -1e30
ML_HEADS = 4
ML_HEAD_DIM = 128
ML_WIDTH = ML_HEADS * ML_HEAD_DIM
ML_CONV = 4
D_MIX = RG_WIDTH + DA_WIDTH + ML_WIDTH
D_FF = 2816
FFN_CONV = 3
EPS = 1e-6
IN_WIDTHS = (RG_WIDTH, RG_WIDTH, DA_WIDTH, DA_WIDTH, DA_WIDTH,
             ML_WIDTH, ML_WIDTH, ML_WIDTH, ML_WIDTH, ML_HEADS, ML_HEADS)
D_IN = sum(IN_WIDTHS)

LANES = 128
SUBLANES = 8
BF16_SUBLANES = 16
D_MAIN = D_IN - 2 * ML_HEADS
ML_Q0 = 2 * RG_WIDTH + 3 * DA_WIDTH
VMEM_LIMIT = 56 * 1024 * 1024

TM_IN = 512
TN_IN = 1536
TM_OUT = 1024
TT_RG = 1024
SCAN_BLOCK = SUBLANES * SUBLANES
TQ = 256
TK = 256
HEADS_PER_STEP = 2
L_ML = 256
TM_MLP = 1024
CH_FF = 256
RB_FF = 128


def _params(n_axes):
    return pltpu.CompilerParams(dimension_semantics=("arbitrary",) * n_axes,
                                vmem_limit_bytes=VMEM_LIMIT)


def _full(shape):
    nd = len(shape)
    return pl.BlockSpec(shape, lambda *_: (0,) * nd)


def _layer(arr, l):
    nd = arr.ndim
    return pl.BlockSpec((1,) + arr.shape[1:], lambda *_: (l,) + (0,) * (nd - 1),
                        pipeline_mode=pl.Buffered(1))


def _cast_specs(w, l, steps):
    _, r, c = w.shape
    k = 1
    while steps % k or r % (steps // k) or (r // (steps // k)) % BF16_SUBLANES:
        k *= 2
        assert k <= steps, (r, steps)
    rows = r // (steps // k)
    in_spec = pl.BlockSpec((1, rows, c), lambda i: (l, i // k, 0))
    out_spec = pl.BlockSpec((1, rows, c), lambda i: (0, i // k, 0))
    return in_spec, out_spec, jax.ShapeDtypeStruct((1, r, c), BF16)


def _rows3(arr):
    return arr.reshape(arr.shape[0], 1, arr.shape[1])


def _shift_rows(x, prev8, d):
    rolled = pltpu.roll(x, d, axis=0)
    prolled = pltpu.roll(prev8, d, axis=0)
    row = lax.broadcasted_iota(jnp.int32, prev8.shape, 0)
    top = jnp.where(row < d, prolled, rolled[:SUBLANES])
    return jnp.concatenate([top, rolled[SUBLANES:]], axis=0)


def _causal_conv(x, prev8, w, b):
    k = w.shape[0]
    y = x * w[k - 1:k]
    for d in range(1, k):
        y = y + _shift_rows(x, prev8, d) * w[k - 1 - d:k - d]
    return y + b


def _rms(x, g):
    return x * lax.rsqrt(jnp.mean(x * x, axis=-1, keepdims=True) + EPS) * g


def _sigmoid(x):
    return 0.5 + 0.5 * jnp.tanh(0.5 * x)


def _silu(x):
    h = 0.5 * x
    return h * (1.0 + jnp.tanh(h))


def _inproj_kernel(x_ref, xn_ref, g_ref, w_ref, wg_ref, wu_ref, wd_ref, wo_ref,
                   z_ref, zg_ref, zgt_ref, wub_ref, wdb_ref, wob_ref, ha_ref, hb_ref):
    half = TM_IN // 2
    g = g_ref[0]
    wub_ref[...] = wu_ref[...].astype(BF16)
    wdb_ref[...] = wd_ref[...].astype(BF16)
    wob_ref[...] = wo_ref[...].astype(BF16)

    @pl.when(pl.program_id(0) == 0)
    def _():
        ha_ref[...] = _rms(x_ref[:half], g).astype(BF16)

    def project(h, rows):
        for c0 in range(0, D_MAIN, TN_IN):
            z_ref[rows, c0:c0 + TN_IN] = jnp.dot(h, w_ref[0, :, c0:c0 + TN_IN], preferred_element_type=F32)
        zg = jnp.dot(h, wg_ref[...], preferred_element_type=F32)
        zg_ref[rows, :] = zg
        zgt_ref[:, rows] = zg.T[:SUBLANES]

    hb_ref[...] = _rms(x_ref[half:], g).astype(BF16)
    project(ha_ref[...], slice(0, half))
    ha_next = _rms(xn_ref[...], g).astype(BF16)
    project(hb_ref[...], slice(half, TM_IN))
    ha_ref[...] = ha_next


def _inproj(x2d, g, w_in_b, w_gate, w_up, w_down, w_out, l):
    t, d = x2d.shape
    half = TM_IN // 2
    last_half = t // half - 1
    steps = t // TM_IN
    casts = [_cast_specs(w, l, steps) for w in (w_up, w_down, w_out)]
    return pl.pallas_call(
        _inproj_kernel,
        grid=(steps,),
        in_specs=[pl.BlockSpec((TM_IN, d), lambda i: (i, 0)),
                  pl.BlockSpec((half, d), lambda i: (jnp.minimum(2 * i + 2, last_half), 0)),
                  _layer(g, l), _layer(w_in_b, l), _full((d, LANES))] + [c[0] for c in casts],
        out_specs=[pl.BlockSpec((TM_IN, D_MAIN), lambda i: (i, 0)),
                   pl.BlockSpec((TM_IN, LANES), lambda i: (i, 0)),
                   pl.BlockSpec((SUBLANES, TM_IN), lambda i: (0, i))] + [c[1] for c in casts],
        out_shape=[jax.ShapeDtypeStruct((t, D_MAIN), F32), jax.ShapeDtypeStruct((t, LANES), F32),
                   jax.ShapeDtypeStruct((SUBLANES, t), F32)] + [c[2] for c in casts],
        scratch_shapes=[pltpu.VMEM((half, d), BF16), pltpu.VMEM((half, d), BF16)],
        compiler_params=_params(1),
        name="inproj",
    )(x2d, x2d, g, w_in_b, w_gate, w_up, w_down, w_out)


def _rglru_tile(x, gate, cw_ref, cb_ref, wa_ref, ba_ref, wx_ref, bx_ref, lam_ref, ng_ref,
                prev_ref, carry_ref, a_ref, b_ref, h_ref):
    tt = x.shape[0]
    n_slabs = x.shape[1] // LANES
    u = _causal_conv(x, prev_ref[...], cw_ref[0], cb_ref[0])
    prev_ref[...] = x[tt - SUBLANES:]
    ub = u.astype(BF16)
    ra, ri = [], []
    for n in range(RG_BLOCKS):
        un = ub[:, n * RG_BLOCK:(n + 1) * RG_BLOCK]
        ra.append(jnp.dot(un, wa_ref[0, n], preferred_element_type=F32))
        ri.append(jnp.dot(un, wx_ref[0, n], preferred_element_type=F32))
    r = _sigmoid(jnp.concatenate(ra, axis=-1) + ba_ref[0])
    i = _sigmoid(jnp.concatenate(ri, axis=-1) + bx_ref[0])
    log_a = r * (RG_C * jax.nn.log_sigmoid(lam_ref[0]))
    a = jnp.exp(log_a)
    gain2 = -jnp.tanh(log_a) * (a * a + 1.0)
    gain = jnp.where(gain2 > 0.0, gain2 * lax.rsqrt(gain2), 0.0)
    bt = gain * (i * u)
    for s in range(n_slabs):
        a_ref[s] = a[:, s * LANES:(s + 1) * LANES]
        b_ref[s] = bt[:, s * LANES:(s + 1) * LANES]

    sub = lax.broadcasted_iota(jnp.int32, (SUBLANES, LANES), 0)
    for s in range(n_slabs):
        lanes = slice(s * LANES, (s + 1) * LANES)
        carry = jnp.broadcast_to(carry_ref[:, lanes], (SUBLANES, LANES))
        for base in range(0, tt, SCAN_BLOCK):
            acc_a, acc_h = [], []
            for k in range(SUBLANES):
                rows = pl.ds(base + k, SUBLANES, stride=SUBLANES)
                ak, bk = a_ref[s, rows, :], b_ref[s, rows, :]
                acc_h.append(bk if k == 0 else ak * acc_h[-1] + bk)
                acc_a.append(ak if k == 0 else ak * acc_a[-1])
            seg_a, seg_h = acc_a[-1], acc_h[-1]
            for d in (1, 2, 4):
                keep = sub >= d
                seg_h = jnp.where(keep, seg_a * pltpu.roll(seg_h, d, axis=0) + seg_h, seg_h)
                seg_a = jnp.where(keep, seg_a * pltpu.roll(seg_a, d, axis=0), seg_a)
            seg_end = seg_a * carry + seg_h
            enter = jnp.where(sub == 0, carry, pltpu.roll(seg_end, 1, axis=0))
            for k in range(SUBLANES):
                h_ref[s, pl.ds(base + k, SUBLANES, stride=SUBLANES), :] = acc_h[k] + acc_a[k] * enter
            carry = jnp.broadcast_to(seg_end[SUBLANES - 1:, :], (SUBLANES, LANES))
        carry_ref[:, lanes] = carry[:1]

    h = jnp.concatenate([h_ref[s] for s in range(n_slabs)], axis=-1)
    k1 = math.sqrt(2.0 / math.pi)
    y = (0.5 * gate) * (1.0 + jnp.tanh(gate * (k1 + (k1 * 0.044715) * (gate * gate)))) * h
    return _rms(y, ng_ref[0]).astype(BF16)


def _rglru_kernel(x_ref, gate_ref, cw_ref, cb_ref, wa_ref, ba_ref, wx_ref, bx_ref, lam_ref, ng_ref,
                  y_ref, prev_ref, carry_ref, a_ref, b_ref, h_ref):
    @pl.when(pl.program_id(1) == 0)
    def _():
        prev_ref[...] = jnp.zeros_like(prev_ref)
        carry_ref[...] = jnp.zeros_like(carry_ref)

    y_ref[0] = _rglru_tile(x_ref[0], gate_ref[0], cw_ref, cb_ref, wa_ref, ba_ref, wx_ref, bx_ref, lam_ref, ng_ref,
                           prev_ref, carry_ref, a_ref, b_ref, h_ref)


def _rglru(z, cw, cb, wa, ba, wx, bx, lam, ng, l):
    b, s, _ = z.shape
    w = RG_WIDTH
    slabs = pltpu.VMEM((w // LANES, TT_RG, LANES), F32)
    return pl.pallas_call(
        _rglru_kernel,
        grid=(b, s // TT_RG),
        in_specs=[pl.BlockSpec((1, TT_RG, w), lambda bi, ti: (bi, ti, 0)),
                  pl.BlockSpec((1, TT_RG, w), lambda bi, ti: (bi, ti, 1)),
                  _layer(cw, l), _layer(cb, l), _layer(wa, l), _layer(ba, l), _layer(wx, l), _layer(bx, l),
                  _layer(lam, l), _layer(ng, l)],
        out_specs=pl.BlockSpec((1, TT_RG, w), lambda bi, ti: (bi, ti, 0)),
        out_shape=jax.ShapeDtypeStruct((b, s, w), BF16),
        scratch_shapes=[pltpu.VMEM((SUBLANES, w), F32), pltpu.VMEM((1, w), F32), slabs, slabs, slabs],
        compiler_params=_params(2),
        name="rglru",
    )(z, z, cw, cb, wa, ba, wx, bx, lam, ng)


def _rope_table_kernel(pos_ref, f_ref, c_ref, s_ref):
    ang = pos_ref[0].astype(F32) * f_ref[...]
    cos, sin = jnp.cos(ang), jnp.sin(ang)
    rest = (DA_HEAD_DIM - ROPE_DIM, ang.shape[1])
    c_map = jnp.concatenate([cos, cos, jnp.ones(rest, F32)], axis=0)
    s_map = jnp.concatenate([-sin, sin, jnp.zeros(rest, F32)], axis=0)
    c_ref[0] = jnp.concatenate([c_map, c_map], axis=0).T
    s_ref[0] = jnp.concatenate([s_map, s_map], axis=0).T


def _rope_tables(positions):
    b, s = positions.shape
    half = ROPE_DIM // 2
    inv_freq = ROPE_THETA ** (-jnp.arange(0, ROPE_DIM, 2, dtype=F32) / ROPE_DIM)
    tab = pl.BlockSpec((1, s, 2 * DA_HEAD_DIM), lambda bi: (bi, 0, 0))
    return pl.pallas_call(
        _rope_table_kernel,
        grid=(b,),
        in_specs=[pl.BlockSpec((1, 1, s), lambda bi: (bi, 0, 0)), _full((half, 1))],
        out_specs=[tab, tab],
        out_shape=[jax.ShapeDtypeStruct((b, s, 2 * DA_HEAD_DIM), F32)] * 2,
        compiler_params=_params(1),
        name="rope_tables",
    )(positions.reshape(b, 1, s), inv_freq.reshape(half, 1))


def _rope(x, c, s):
    half = ROPE_DIM // 2
    lane = lax.broadcasted_iota(jnp.int32, (1, LANES), 1)
    first = (lane % DA_HEAD_DIM) < half
    partner = jnp.where(first, pltpu.roll(x, LANES - half, axis=1), pltpu.roll(x, half, axis=1))
    return x * c + partner * s


def _attn_kernel(lam_ref, q_ref, k_ref, v_ref, cos_ref, sin_ref, ng_ref, o_ref,
                 qb_ref, kz_ref, vb_ref, sc_ref, p_ref, *, lambda_init):
    hw = 2 * DA_HEAD_DIM
    for hh in range(q_ref.shape[2] // hw):
        lanes = pl.ds(hh * hw, hw)
        _attn_head(lam_ref, q_ref.at[:, :, lanes], k_ref.at[:, :, lanes], v_ref.at[:, :, lanes], cos_ref, sin_ref,
                   ng_ref, o_ref.at[:, :, lanes], qb_ref.at[hh], kz_ref.at[hh], vb_ref.at[hh], sc_ref.at[hh],
                   p_ref.at[hh], lambda_init=lambda_init)


def _attn_head(lam_ref, q_ref, k_ref, v_ref, cos_ref, sin_ref, ng_ref, o_ref,
               qb_ref, kz_ref, vb_ref, sc_ref, p_ref, *, lambda_init):
    s_len = k_ref.shape[1]
    nt = s_len // TK
    groups = TK // LANES
    lane = lax.broadcasted_iota(jnp.int32, (1, LANES), 1)
    map0 = lane < DA_HEAD_DIM
    q_scale = DA_HEAD_DIM ** -0.5 * math.log2(math.e)

    for j in range(nt):
        rows = slice(j * TK, (j + 1) * TK)
        cos, sin = cos_ref[0, rows, :], sin_ref[0, rows, :]
        kr = _rope(k_ref[0, rows, :], cos, sin)
        kz_ref[0, rows, :] = jnp.where(map0, kr, 0.0).astype(BF16)
        kz_ref[1, rows, :] = jnp.where(map0, 0.0, kr).astype(BF16)
        vb_ref[rows, :] = jnp.concatenate(
            [v_ref[0, rows, :], jnp.broadcast_to(jnp.where(lane == 0, 1.0, 0.0), (TK, LANES))], axis=-1).astype(BF16)
        qb_ref[rows, :] = (_rope(q_ref[0, rows, :], cos, sin) * q_scale).astype(BF16)

    lp = lam_ref[0]
    lam = (jnp.exp(jnp.sum(lp[0:1] * lp[1:2], axis=-1, keepdims=True))
           - jnp.exp(jnp.sum(lp[2:3] * lp[3:4], axis=-1, keepdims=True)) + lambda_init)
    on_or_below_diag = (lax.broadcasted_iota(jnp.int32, (TQ, TK), 0)
                        >= lax.broadcasted_iota(jnp.int32, (TQ, TK), 1))

    for qi in reversed(range(nt)):
        kv = (qi + 1) * TK
        q = qb_ref[qi * TQ:(qi + 1) * TQ, :]
        for c in range(2):
            sc_ref[c, :, :kv] = lax.dot_general(q, kz_ref[c, :kv, :], (((1,), (1,)), ((), ())),
                                                preferred_element_type=F32)
        for c in range(2):
            m = None
            for j in range(qi + 1):
                cols = slice(j * TK, (j + 1) * TK)
                s = sc_ref[c, :, cols]
                if j == qi:
                    s = jnp.where(on_or_below_diag, s, NEG_INF)
                    sc_ref[c, :, cols] = s
                for g in range(groups):
                    sg = s[:, g * LANES:(g + 1) * LANES]
                    m = sg if m is None else jnp.maximum(m, sg)
            m_b = jnp.broadcast_to(jnp.max(m, axis=-1, keepdims=True), (TQ, LANES))
            m_b = jnp.concatenate([m_b] * groups, axis=-1)
            for j in range(qi + 1):
                cols = slice(j * TK, (j + 1) * TK)
                p_ref[c * TQ:(c + 1) * TQ, cols] = jnp.exp2(sc_ref[c, :, cols] - m_b).astype(BF16)
        acc = jnp.dot(p_ref[:, :kv], vb_ref[:kv, :], preferred_element_type=F32)
        hw = 2 * DA_HEAD_DIM
        o = (acc[:TQ, :hw] / acc[:TQ, hw:hw + 1]) - lam * (acc[TQ:, :hw] / acc[TQ:, hw:hw + 1])
        o_ref[0, qi * TQ:(qi + 1) * TQ, :] = (_rms(o, ng_ref[0]) * (1.0 - lambda_init)).astype(BF16)


def _attention(z, cos_t, sin_t, lam_p, ng, lambda_init, l):
    b, s, _ = z.shape
    hw = 2 * DA_HEAD_DIM
    q0 = (2 * RG_WIDTH) // hw
    k0 = q0 + DA_HEADS
    v0 = k0 + DA_HEADS
    nh = HEADS_PER_STEP
    assert q0 % nh == 0 and DA_HEADS % nh == 0
    seq = lambda c0: pl.BlockSpec((1, s, nh * hw), lambda bi, hi: (bi, 0, c0 // nh + hi))
    tab = pl.BlockSpec((1, s, hw), lambda bi, hi: (bi, 0, 0))
    return pl.pallas_call(
        functools.partial(_attn_kernel, lambda_init=lambda_init),
        grid=(b, DA_HEADS // nh),
        in_specs=[_layer(lam_p, l), seq(q0), seq(k0), seq(v0), tab, tab, _layer(ng, l)],
        out_specs=pl.BlockSpec((1, s, nh * hw), lambda bi, hi: (bi, 0, hi)),
        out_shape=jax.ShapeDtypeStruct((b, s, DA_WIDTH), BF16),
        scratch_shapes=[pltpu.VMEM((nh, s, hw), BF16), pltpu.VMEM((nh, 2, s, hw), BF16),
                        pltpu.VMEM((nh, s, 2 * hw), BF16), pltpu.VMEM((nh, 2, TQ, s), F32),
                        pltpu.VMEM((nh, 2 * TQ, s), BF16)],
        compiler_params=_params(2),
        name="diffattn",
    )(lam_p, z, z, z, cos_t, sin_t, ng)


def _mlstm_kernel(q_ref, k_ref, v_ref, o_ref, gcol_ref, grow_ref, cw_ref, cb_ref, brow_ref, bcol_ref,
                  ng_ref, y_ref, pq_ref, pk_ref, c_ref, m_ref):
    @pl.when(pl.program_id(1) == 0)
    def _():
        pq_ref[...] = jnp.zeros_like(pq_ref)
        pk_ref[...] = jnp.zeros_like(pk_ref)
        c_ref[...] = jnp.zeros_like(c_ref)
        m_ref[...] = jnp.zeros_like(m_ref)

    q_raw, k_raw = q_ref[0], k_ref[0]
    y_ref[0] = _mlstm_chunk(q_raw, k_raw, pq_ref[...], pk_ref[...], v_ref[0], o_ref[0], gcol_ref[0],
                            grow_ref[...], cw_ref, cb_ref, brow_ref, bcol_ref, ng_ref, c_ref, m_ref)
    pq_ref[...] = q_raw[L_ML - SUBLANES:]
    pk_ref[...] = k_raw[L_ML - SUBLANES:]


def _mlstm_chunk(q_raw, k_raw, pq, pk, vv, o_pre, gcol, grow, cw_ref, cb_ref, brow_ref, bcol_ref, ng_ref,
                 c_ref, m_ref):
    w = ML_WIDTH
    dh = ML_HEAD_DIM
    ln = q_raw.shape[0]
    qs = _silu(_causal_conv(q_raw, pq, cw_ref[0, :, :w], cb_ref[0, :, :w]))
    ks = _silu(_causal_conv(k_raw, pk, cw_ref[0, :, w:], cb_ref[0, :, w:])) * (dh ** -0.5)
    og = _sigmoid(o_pre)

    gc = gcol + brow_ref[...]
    gr = grow + bcol_ref[...]
    r_i = lax.broadcasted_iota(jnp.int32, (ln, ln), 0)
    c_i = lax.broadcasted_iota(jnp.int32, (ln, ln), 1)
    causal = r_i >= c_i
    b_cols = jnp.dot(causal.astype(F32), jax.nn.log_sigmoid(gc),
                     precision=lax.Precision.HIGHEST, preferred_element_type=F32)
    b_rows = jnp.dot(jax.nn.log_sigmoid(gr), (r_i <= c_i).astype(F32),
                     precision=lax.Precision.HIGHEST, preferred_element_type=F32)

    lane = lax.broadcasted_iota(jnp.int32, (1, LANES), 1)
    sub = lax.broadcasted_iota(jnp.int32, (SUBLANES, 1), 0)
    t_col = lax.broadcasted_iota(jnp.int32, (ln, 1), 0)

    def lane_pick(a, j):
        return jnp.sum(jnp.where(lane == j, a, 0.0), axis=-1, keepdims=True)

    def row_pick(a, j):
        return jnp.sum(jnp.where(sub == j, a, 0.0), axis=0, keepdims=True)

    outs = []
    for hh in range(ML_HEADS):
        sl = slice(hh * dh, (hh + 1) * dh)
        b_col = lane_pick(b_cols, ML_HEADS + hh)
        li_col = lane_pick(gc, hh)
        r_row = row_pick(gr, hh) - row_pick(b_rows, ML_HEADS + hh)
        m_prev = m_ref[hh]
        qh = qs[:, sl]
        kh = ks[:, sl]
        qb = qh.astype(BF16)
        kb = kh.astype(BF16)
        vb = jnp.concatenate([vv[:, sl].astype(BF16), jnp.ones((ln, dh), BF16)], axis=-1)
        c_prev = c_ref[hh]
        c_prev_b = c_prev.astype(BF16)

        def rows_out(r0, r1):
            b_part = b_col[r0:r1]
            m_inter = b_part + m_prev
            dmat = jnp.where(causal[r0:r1, :r1], b_part + r_row[:, :r1], -jnp.inf)
            m_t = jnp.maximum(m_inter, jnp.max(dmat, axis=-1, keepdims=True))
            w_intra = lax.dot_general(qb[r0:r1], kb[:r1], (((1,), (1,)), ((), ())),
                                      preferred_element_type=F32) * jnp.exp(dmat - m_t)
            num_den = (jnp.exp(m_inter - m_t) * jnp.dot(qb[r0:r1], c_prev_b, preferred_element_type=F32)
                       + jnp.dot(w_intra.astype(BF16), vb[:r1], preferred_element_type=F32))
            return num_den[:, :dh] / jnp.maximum(jnp.abs(num_den[:, dh:]), jnp.exp(-m_t))

        hout = jnp.concatenate([rows_out(0, ln // 2), rows_out(ln // 2, ln)], axis=0)
        outs.append(_rms(hout, ng_ref[0, :, sl]))
        b_last = jnp.sum(jnp.where(t_col == ln - 1, b_col, 0.0), axis=0, keepdims=True)
        g_col = b_last - b_col + li_col
        m_next = jnp.maximum(b_last + m_prev, jnp.max(g_col, axis=0, keepdims=True))
        decay = jnp.exp(b_last + m_prev - m_next)
        kw = kh * jnp.exp(g_col - m_next)
        c_ref[hh] = decay * c_prev + lax.dot_general(kw.astype(BF16), vb, (((0,), (0,)), ((), ())),
                                                     preferred_element_type=F32)
        m_ref[hh] = m_next
    return (jnp.concatenate(outs, axis=-1) * og).astype(BF16)


def _mlstm(z, zg, zgt, cw, cb, brow, bcol, ng, l):
    b, s, _ = z.shape
    w = ML_WIDTH
    nc = s // L_ML
    c0 = ML_Q0 // w
    col = lambda c: pl.BlockSpec((1, L_ML, w), lambda bi, ci: (bi, ci, c))
    return pl.pallas_call(
        _mlstm_kernel,
        grid=(b, nc),
        in_specs=[col(c0), col(c0 + 1), col(c0 + 2), col(c0 + 3),
                  pl.BlockSpec((1, L_ML, LANES), lambda bi, ci: (bi, ci, 0)),
                  pl.BlockSpec((SUBLANES, L_ML), lambda bi, ci: (0, bi * nc + ci)),
                  _layer(cw, l), _layer(cb, l), _full((1, LANES)), _full((SUBLANES, 1)), _layer(ng, l)],
        out_specs=col(0),
        out_shape=jax.ShapeDtypeStruct((b, s, w), BF16),
        scratch_shapes=[pltpu.VMEM((SUBLANES, w), F32), pltpu.VMEM((SUBLANES, w), F32),
                        pltpu.VMEM((ML_HEADS, ML_HEAD_DIM, 2 * ML_HEAD_DIM), F32),
                        pltpu.VMEM((ML_HEADS, 1, 1), F32)],
        compiler_params=_params(2),
        name="mlstm",
    )(z, z, z, z, zg, zgt, cw, cb, brow, bcol, ng)


def _outproj_kernel(x_ref, yr_ref, ya_ref, ym_ref, w_ref, g_ref, x1_ref, h_ref):
    half = TM_OUT // 2
    for r0 in (0, half):
        rows = slice(r0, r0 + half)
        acc = x_ref[rows, :]
        acc = acc + jnp.dot(yr_ref[rows, :], w_ref[0, 0:RG_WIDTH, :], preferred_element_type=F32)
        acc = acc + jnp.dot(ya_ref[rows, :], w_ref[0, RG_WIDTH:RG_WIDTH + DA_WIDTH, :], preferred_element_type=F32)
        acc = acc + jnp.dot(ym_ref[rows, :], w_ref[0, RG_WIDTH + DA_WIDTH:, :], preferred_element_type=F32)
        x1_ref[rows, :] = acc
        h_ref[rows, :] = _rms(acc, g_ref[0]).astype(BF16)


def _outproj(x2d, y_rg, y_da, y_ml, w_out, g, l):
    t, d = x2d.shape
    row = lambda wd: pl.BlockSpec((TM_OUT, wd), lambda i: (i, 0))
    return pl.pallas_call(
        _outproj_kernel,
        grid=(t // TM_OUT,),
        in_specs=[row(d), row(RG_WIDTH), row(DA_WIDTH), row(ML_WIDTH), _layer(w_out, 0), _layer(g, l)],
        out_specs=[row(d), row(d)],
        out_shape=[jax.ShapeDtypeStruct((t, d), F32), jax.ShapeDtypeStruct((t, d), BF16)],
        compiler_params=_params(1),
        name="outproj",
    )(x2d, y_rg, y_da, y_ml, w_out, g)


def _mlp_kernel(h_ref, x_ref, wup_ref, cw_ref, cb_ref, wdn_ref, ng_ref, o_ref, tail_ref, u_ref, act_ref,
                hs_ref, *, final_norm):
    @pl.when(pl.program_id(1) == 0)
    def _():
        tail_ref[...] = jnp.zeros_like(tail_ref)

    tm = h_ref.shape[1]
    hs_ref[...] = h_ref[0]
    n_ch = D_FF // CH_FF
    assert n_ch % 2 == 1 and n_ch * CH_FF == D_FF

    def cols(half, j):
        return pl.ds(half * D_FF + j * CH_FF, CH_FF)

    def up(j, slot):
        for half in range(2):
            u_ref[slot, half] = jnp.dot(hs_ref[...], wup_ref[0, :, cols(half, j)], preferred_element_type=F32)

    def gate(j, slot):
        out_cols = pl.ds(j * CH_FF, CH_FF)
        taps = [(cw_ref[0, :, cols(half, j)] * sc, cb_ref[0, :, cols(half, j)] * sc) for half, sc in ((0, 0.5), (1, 1.0))]
        for r0 in range(0, tm, RB_FF):
            halves = []
            for half in range(2):
                prev = tail_ref[:, cols(half, j)] if r0 == 0 else u_ref[slot, half, r0 - SUBLANES:r0, :]
                halves.append(_causal_conv(u_ref[slot, half, r0:r0 + RB_FF, :], prev, *taps[half]))
            g_half = halves[0]
            act_ref[r0:r0 + RB_FF, out_cols] = (g_half * (1.0 + jnp.tanh(g_half)) * halves[1]).astype(BF16)
        for half in range(2):
            tail_ref[:, cols(half, j)] = u_ref[slot, half, tm - SUBLANES:, :]

    up(0, 0)
    for j in range(n_ch):
        if j + 1 < n_ch:
            up(j + 1, (j + 1) % 2)
        gate(j, j % 2)

    y = x_ref[0] + jnp.dot(act_ref[...], wdn_ref[0], preferred_element_type=F32)
    if final_norm:
        y = _rms(y, ng_ref[...])
    o_ref[0] = y


def _mlp(h, x1, w_up_b, cw, cb, w_down_b, ng, final_norm, l):
    b, s, d = x1.shape
    tile = pl.BlockSpec((1, TM_MLP, d), lambda bi, ti: (bi, ti, 0))
    return pl.pallas_call(
        functools.partial(_mlp_kernel, final_norm=final_norm),
        grid=(b, s // TM_MLP),
        in_specs=[tile, tile, _layer(w_up_b, 0), _layer(cw, l), _layer(cb, l), _layer(w_down_b, 0), _full((1, d))],
        out_specs=tile,
        out_shape=jax.ShapeDtypeStruct((b, s, d), F32),
        scratch_shapes=[pltpu.VMEM((SUBLANES, 2 * D_FF), F32), pltpu.VMEM((2, 2, TM_MLP, CH_FF), F32),
                        pltpu.VMEM((TM_MLP, D_FF), BF16), pltpu.VMEM((TM_MLP, d), BF16)],
        compiler_params=_params(2),
        name="mlp",
    )(h, x1, w_up_b, cw, cb, w_down_b, ng)


def kernel(x, positions, attn_norm, w_in, rg_conv_w, rg_conv_b, rg_wa, rg_ba, rg_wx, rg_bx, rg_lambda, rg_norm, da_lambda, da_norm, ml_conv_w, ml_conv_b, ml_i_bias, ml_f_bias, ml_norm, w_out, mlp_norm, w_up, ffn_conv_w, ffn_conv_b, w_down, final_norm):
    b, s, d = x.shape
    depth = w_in.shape[0]
    t = b * s
    n_gate = 2 * ML_HEADS
    cos_t, sin_t = _rope_tables(positions)
    w_in_b, wa_b, wx_b = w_in.astype(BF16), rg_wa.astype(BF16), rg_wx.astype(BF16)
    w_gate_b = jnp.pad(w_in[:, :, D_MAIN:], ((0, 0), (0, 0), (0, LANES - n_gate))).astype(BF16)
    gate_bias = jnp.concatenate([ml_i_bias, ml_f_bias], axis=-1)
    attn_norm3, mlp_norm3, rg_norm3, da_norm3, ml_norm3 = map(_rows3, (attn_norm, mlp_norm, rg_norm, da_norm, ml_norm))
    rg_cb3, rg_ba3, rg_bx3, rg_lam3, ml_cb3, ffn_cb3 = map(_rows3, (rg_conv_b, rg_ba, rg_bx, rg_lambda, ml_conv_b, ffn_conv_b))
    for l in range(depth):
        lambda_init = 0.8 - 0.6 * math.exp(-0.3 * l)
        z, zg, zgt, w_up_b, w_down_b, w_out_b = _inproj(x.reshape(t, d), attn_norm3, w_in_b, w_gate_b[l],
                                                        w_up, w_down, w_out, l)
        z = z.reshape(b, s, D_MAIN)
        zg = zg.reshape(b, s, LANES)
        y_rg = _rglru(z, rg_conv_w, rg_cb3, wa_b, rg_ba3, wx_b, rg_bx3, rg_lam3, rg_norm3, l)
        y_da = _attention(z, cos_t, sin_t, da_lambda, da_norm3, lambda_init, l)
        y_ml = _mlstm(z, zg, zgt, ml_conv_w, ml_cb3,
                      jnp.pad(gate_bias[l], (0, LANES - n_gate)).reshape(1, LANES),
                      gate_bias[l].reshape(n_gate, 1), ml_norm3, l)
        x1, h2 = _outproj(x.reshape(t, d), y_rg.reshape(t, -1), y_da.reshape(t, -1), y_ml.reshape(t, -1),
                          w_out_b, mlp_norm3, l)
        x = _mlp(h2.reshape(b, s, d), x1.reshape(b, s, d), w_up_b, ffn_conv_w, ffn_cb3, w_down_b,
                 final_norm.reshape(1, d), l == depth - 1, l)
    return x
```

```python
import functools
import math

import jax
import jax.numpy as jnp
from jax import lax
from jax.experimental import pallas as pl
from jax.experimental.pallas import tpu as pltpu

F32 = jnp.float32
BF16 = jnp.bfloat16

D_MODEL = 1024
RG_WIDTH = 512
RG_BLOCKS = 4
RG_BLOCK = RG_WIDTH // RG_BLOCKS
RG_CONV = 4
RG_C = 8.0
DA_HEADS = 4
DA_HEAD_DIM = 64
DA_WIDTH = DA_HEADS * 2 * DA_HEAD_DIM
ROPE_THETA = 500000.0
ROPE_DIM = DA_HEAD_DIM // 4
NEG_INF = -1e30
ML_HEADS = 4
ML_HEAD_DIM = 128
ML_WIDTH = ML_HEADS * ML_HEAD_DIM
ML_CONV = 4
D_MIX = RG_WIDTH + DA_WIDTH + ML_WIDTH
D_FF = 2816
FFN_CONV = 3
EPS = 1e-6
IN_WIDTHS = (RG_WIDTH, RG_WIDTH, DA_WIDTH, DA_WIDTH, DA_WIDTH,
             ML_WIDTH, ML_WIDTH, ML_WIDTH, ML_WIDTH, ML_HEADS, ML_HEADS)
D_IN = sum(IN_WIDTHS)

LANES = 128
SUBLANES = 8
BF16_SUBLANES = 16
D_MAIN = D_IN - 2 * ML_HEADS
ML_Q0 = 2 * RG_WIDTH + 3 * DA_WIDTH
VMEM_LIMIT = 56 * 1024 * 1024

TM_IN = 512
TN_IN = 1536
TM_OUT = 1024
TT_RG = 1024
RG_SUB = 256
SCAN_BLOCK = SUBLANES * SUBLANES
TQ = 256
TK = 256
HEADS_PER_STEP = 2
L_ML = 256
TM_MLP = 1024
CH_FF = 256
RB_FF = 128


def _params(n_axes):
    return pltpu.CompilerParams(dimension_semantics=("arbitrary",) * n_axes,
                                vmem_limit_bytes=VMEM_LIMIT)


def _full(shape):
    nd = len(shape)
    return pl.BlockSpec(shape, lambda *_: (0,) * nd)


def _layer(arr, l):
    nd = arr.ndim
    return pl.BlockSpec((1,) + arr.shape[1:], lambda *_: (l,) + (0,) * (nd - 1),
                        pipeline_mode=pl.Buffered(1))


def _cast_specs(w, l, steps):
    _, r, c = w.shape
    k = 1
    while steps % k or r % (steps // k) or (r // (steps // k)) % BF16_SUBLANES:
        k *= 2
        assert k <= steps, (r, steps)
    rows = r // (steps // k)
    in_spec = pl.BlockSpec((1, rows, c), lambda i: (l, i // k, 0))
    out_spec = pl.BlockSpec((1, rows, c), lambda i: (0, i // k, 0))
    return in_spec, out_spec, jax.ShapeDtypeStruct((1, r, c), BF16)


def _rows3(arr):
    return arr.reshape(arr.shape[0], 1, arr.shape[1])


def _shift_rows(x, prev8, d):
    rolled = pltpu.roll(x, d, axis=0)
    prolled = pltpu.roll(prev8, d, axis=0)
    row = lax.broadcasted_iota(jnp.int32, prev8.shape, 0)
    top = jnp.where(row < d, prolled, rolled[:SUBLANES])
    return jnp.concatenate([top, rolled[SUBLANES:]], axis=0)


def _causal_conv(x, prev8, w, b):
    k = w.shape[0]
    y = x * w[k - 1:k]
    for d in range(1, k):
        y = y + _shift_rows(x, prev8, d) * w[k - 1 - d:k - d]
    return y + b


def _rms(x, g):
    return x * lax.rsqrt(jnp.mean(x * x, axis=-1, keepdims=True) + EPS) * g


def _sigmoid(x):
    return 0.5 + 0.5 * jnp.tanh(0.5 * x)


def _silu(x):
    h = 0.5 * x
    return h * (1.0 + jnp.tanh(h))


def _inproj_kernel(x_ref, xn_ref, g_ref, w_ref, wg_ref, wu_ref, wd_ref, wo_ref,
                   z_ref, zg_ref, zgt_ref, wub_ref, wdb_ref, wob_ref, ha_ref, hb_ref):
    half = TM_IN // 2
    g = g_ref[0]
    wub_ref[...] = wu_ref[...].astype(BF16)
    wdb_ref[...] = wd_ref[...].astype(BF16)
    wob_ref[...] = wo_ref[...].astype(BF16)

    @pl.when(pl.program_id(0) == 0)
    def _():
        ha_ref[...] = _rms(x_ref[:half], g).astype(BF16)

    def project(h, rows):
        for c0 in range(0, D_MAIN, TN_IN):
            z_ref[rows, c0:c0 + TN_IN] = jnp.dot(h, w_ref[0, :, c0:c0 + TN_IN], preferred_element_type=F32)
        zg = jnp.dot(h, wg_ref[...], preferred_element_type=F32)
        zg_ref[rows, :] = zg
        zgt_ref[:, rows] = zg.T[:SUBLANES]

    hb_ref[...] = _rms(x_ref[half:], g).astype(BF16)
    project(ha_ref[...], slice(0, half))
    ha_next = _rms(xn_ref[...], g).astype(BF16)
    project(hb_ref[...], slice(half, TM_IN))
    ha_ref[...] = ha_next


def _inproj(x2d, g, w_in_b, w_gate, w_up, w_down, w_out, l):
    t, d = x2d.shape
    half = TM_IN // 2
    last_half = t // half - 1
    steps = t // TM_IN
    casts = [_cast_specs(w, l, steps) for w in (w_up, w_down, w_out)]
    return pl.pallas_call(
        _inproj_kernel,
        grid=(steps,),
        in_specs=[pl.BlockSpec((TM_IN, d), lambda i: (i, 0)),
                  pl.BlockSpec((half, d), lambda i: (jnp.minimum(2 * i + 2, last_half), 0)),
                  _layer(g, l), _layer(w_in_b, l), _full((d, LANES))] + [c[0] for c in casts],
        out_specs=[pl.BlockSpec((TM_IN, D_MAIN), lambda i: (i, 0)),
                   pl.BlockSpec((TM_IN, LANES), lambda i: (i, 0)),
                   pl.BlockSpec((SUBLANES, TM_IN), lambda i: (0, i))] + [c[1] for c in casts],
        out_shape=[jax.ShapeDtypeStruct((t, D_MAIN), F32), jax.ShapeDtypeStruct((t, LANES), F32),
                   jax.ShapeDtypeStruct((SUBLANES, t), F32)] + [c[2] for c in casts],
        scratch_shapes=[pltpu.VMEM((half, d), BF16), pltpu.VMEM((half, d), BF16)],
        compiler_params=_params(1),
        name="inproj",
    )(x2d, x2d, g, w_in_b, w_gate, w_up, w_down, w_out)


def _rglru_tile(x, gate, cw_ref, cb_ref, wa_ref, ba_ref, wx_ref, bx_ref, lam_ref, ng_ref,
                prev_ref, carry_ref, a_ref, b_ref, h_ref):
    tt = x.shape[0]
    n_slabs = x.shape[1] // LANES
    u = _causal_conv(x, prev_ref[...], cw_ref[0], cb_ref[0])
    prev_ref[...] = x[tt - SUBLANES:]
    ub = u.astype(BF16)
    ra, ri = [], []
    for n in range(RG_BLOCKS):
        un = ub[:, n * RG_BLOCK:(n + 1) * RG_BLOCK]
        ra.append(jnp.dot(un, wa_ref[0, n], preferred_element_type=F32))
        ri.append(jnp.dot(un, wx_ref[0, n], preferred_element_type=F32))
    r = _sigmoid(jnp.concatenate(ra, axis=-1) + ba_ref[0])
    i = _sigmoid(jnp.concatenate(ri, axis=-1) + bx_ref[0])
    log_a = r * (RG_C * jax.nn.log_sigmoid(lam_ref[0]))
    a = jnp.exp(log_a)
    gain2 = -jnp.tanh(log_a) * (a * a + 1.0)
    gain = jnp.where(gain2 > 0.0, gain2 * lax.rsqrt(gain2), 0.0)
    bt = gain * (i * u)
    for s in range(n_slabs):
        a_ref[s] = a[:, s * LANES:(s + 1) * LANES]
        b_ref[s] = bt[:, s * LANES:(s + 1) * LANES]

    sub = lax.broadcasted_iota(jnp.int32, (SUBLANES, LANES), 0)
    for s in range(n_slabs):
        lanes = slice(s * LANES, (s + 1) * LANES)
        carry = jnp.broadcast_to(carry_ref[:, lanes], (SUBLANES, LANES))
        for base in range(0, tt, SCAN_BLOCK):
            acc_a, acc_h = [], []
            for k in range(SUBLANES):
                rows = pl.ds(base + k, SUBLANES, stride=SUBLANES)
                ak, bk = a_ref[s, rows, :], b_ref[s, rows, :]
                acc_h.append(bk if k == 0 else ak * acc_h[-1] + bk)
                acc_a.append(ak if k == 0 else ak * acc_a[-1])
            seg_a, seg_h = acc_a[-1], acc_h[-1]
            for d in (1, 2, 4):
                keep = sub >= d
                seg_h = jnp.where(keep, seg_a * pltpu.roll(seg_h, d, axis=0) + seg_h, seg_h)
                seg_a = jnp.where(keep, seg_a * pltpu.roll(seg_a, d, axis=0), seg_a)
            seg_end = seg_a * carry + seg_h
            enter = jnp.where(sub == 0, carry, pltpu.roll(seg_end, 1, axis=0))
            for k in range(SUBLANES):
                h_ref[s, pl.ds(base + k, SUBLANES, stride=SUBLANES), :] = acc_h[k] + acc_a[k] * enter
            carry = jnp.broadcast_to(seg_end[SUBLANES - 1:, :], (SUBLANES, LANES))
        carry_ref[:, lanes] = carry[:1]

    h = jnp.concatenate([h_ref[s] for s in range(n_slabs)], axis=-1)
    k1 = math.sqrt(2.0 / math.pi)
    y = (0.5 * gate) * (1.0 + jnp.tanh(gate * (k1 + (k1 * 0.044715) * (gate * gate)))) * h
    return _rms(y, ng_ref[0]).astype(BF16)


def _rglru_kernel(x_ref, gate_ref, cw_ref, cb_ref, wa_ref, ba_ref, wx_ref, bx_ref, lam_ref, ng_ref,
                  y_ref, prev_ref, carry_ref, a_ref, b_ref, h_ref):
    @pl.when(pl.program_id(1) == 0)
    def _():
        prev_ref[...] = jnp.zeros_like(prev_ref)
        carry_ref[...] = jnp.zeros_like(carry_ref)

    for r0 in range(0, x_ref.shape[1], RG_SUB):
        rows = slice(r0, r0 + RG_SUB)
        y_ref[0, rows, :] = _rglru_tile(x_ref[0, rows, :], gate_ref[0, rows, :], cw_ref, cb_ref, wa_ref, ba_ref,
                                        wx_ref, bx_ref, lam_ref, ng_ref, prev_ref, carry_ref, a_ref, b_ref, h_ref)


def _rglru(z, cw, cb, wa, ba, wx, bx, lam, ng, l):
    b, s, _ = z.shape
    w = RG_WIDTH
    slabs = pltpu.VMEM((w // LANES, RG_SUB, LANES), F32)
    return pl.pallas_call(
        _rglru_kernel,
        grid=(b, s // TT_RG),
        in_specs=[pl.BlockSpec((1, TT_RG, w), lambda bi, ti: (bi, ti, 0)),
                  pl.BlockSpec((1, TT_RG, w), lambda bi, ti: (bi, ti, 1)),
                  _layer(cw, l), _layer(cb, l), _layer(wa, l), _layer(ba, l), _layer(wx, l), _layer(bx, l),
                  _layer(lam, l), _layer(ng, l)],
        out_specs=pl.BlockSpec((1, TT_RG, w), lambda bi, ti: (bi, ti, 0)),
        out_shape=jax.ShapeDtypeStruct((b, s, w), BF16),
        scratch_shapes=[pltpu.VMEM((SUBLANES, w), F32), pltpu.VMEM((1, w), F32), slabs, slabs, slabs],
        compiler_params=_params(2),
        name="rglru",
    )(z, z, cw, cb, wa, ba, wx, bx, lam, ng)


def _rope_table_kernel(pos_ref, f_ref, c_ref, s_ref):
    ang = pos_ref[0].astype(F32) * f_ref[...]
    cos, sin = jnp.cos(ang), jnp.sin(ang)
    rest = (DA_HEAD_DIM - ROPE_DIM, ang.shape[1])
    c_map = jnp.concatenate([cos, cos, jnp.ones(rest, F32)], axis=0)
    s_map = jnp.concatenate([-sin, sin, jnp.zeros(rest, F32)], axis=0)
    c_ref[0] = jnp.concatenate([c_map, c_map], axis=0).T
    s_ref[0] = jnp.concatenate([s_map, s_map], axis=0).T


def _rope_tables(positions):
    b, s = positions.shape
    half = ROPE_DIM // 2
    inv_freq = ROPE_THETA ** (-jnp.arange(0, ROPE_DIM, 2, dtype=F32) / ROPE_DIM)
    tab = pl.BlockSpec((1, s, 2 * DA_HEAD_DIM), lambda bi: (bi, 0, 0))
    return pl.pallas_call(
        _rope_table_kernel,
        grid=(b,),
        in_specs=[pl.BlockSpec((1, 1, s), lambda bi: (bi, 0, 0)), _full((half, 1))],
        out_specs=[tab, tab],
        out_shape=[jax.ShapeDtypeStruct((b, s, 2 * DA_HEAD_DIM), F32)] * 2,
        compiler_params=_params(1),
        name="rope_tables",
    )(positions.reshape(b, 1, s), inv_freq.reshape(half, 1))


def _rope(x, c, s):
    half = ROPE_DIM // 2
    lane = lax.broadcasted_iota(jnp.int32, (1, LANES), 1)
    first = (lane % DA_HEAD_DIM) < half
    partner = jnp.where(first, pltpu.roll(x, LANES - half, axis=1), pltpu.roll(x, half, axis=1))
    return x * c + partner * s


def _attn_kernel(lam_ref, q_ref, k_ref, v_ref, cos_ref, sin_ref, ng_ref, o_ref,
                 qb_ref, kz_ref, vb_ref, sc_ref, p_ref, *, lambda_init):
    hw = 2 * DA_HEAD_DIM
    for hh in range(q_ref.shape[2] // hw):
        lanes = pl.ds(hh * hw, hw)
        _attn_head(lam_ref, q_ref.at[:, :, lanes], k_ref.at[:, :, lanes], v_ref.at[:, :, lanes], cos_ref, sin_ref,
                   ng_ref, o_ref.at[:, :, lanes], qb_ref.at[hh], kz_ref.at[hh], vb_ref.at[hh], sc_ref.at[hh],
                   p_ref.at[hh], lambda_init=lambda_init)


def _attn_head(lam_ref, q_ref, k_ref, v_ref, cos_ref, sin_ref, ng_ref, o_ref,
               qb_ref, kz_ref, vb_ref, sc_ref, p_ref, *, lambda_init):
    s_len = k_ref.shape[1]
    nt = s_len // TK
    groups = TK // LANES
    lane = lax.broadcasted_iota(jnp.int32, (1, LANES), 1)
    map0 = lane < DA_HEAD_DIM
    q_scale = DA_HEAD_DIM ** -0.5 * math.log2(math.e)

    for j in range(nt):
        rows = slice(j * TK, (j + 1) * TK)
        cos, sin = cos_ref[0, rows, :], sin_ref[0, rows, :]
        kr = _rope(k_ref[0, rows, :], cos, sin)
        kz_ref[0, rows, :] = jnp.where(map0, kr, 0.0).astype(BF16)
        kz_ref[1, rows, :] = jnp.where(map0, 0.0, kr).astype(BF16)
        vb_ref[rows, :] = jnp.concatenate(
            [v_ref[0, rows, :], jnp.broadcast_to(jnp.where(lane == 0, 1.0, 0.0), (TK, LANES))], axis=-1).astype(BF16)
        qb_ref[rows, :] = (_rope(q_ref[0, rows, :], cos, sin) * q_scale).astype(BF16)

    lp = lam_ref[0]
    lam = (jnp.exp(jnp.sum(lp[0:1] * lp[1:2], axis=-1, keepdims=True))
           - jnp.exp(jnp.sum(lp[2:3] * lp[3:4], axis=-1, keepdims=True)) + lambda_init)
    on_or_below_diag = (lax.broadcasted_iota(jnp.int32, (TQ, TK), 0)
                        >= lax.broadcasted_iota(jnp.int32, (TQ, TK), 1))

    for qi in reversed(range(nt)):
        kv = (qi + 1) * TK
        q = qb_ref[qi * TQ:(qi + 1) * TQ, :]
        for c in range(2):
            sc_ref[c, :, :kv] = lax.dot_general(q, kz_ref[c, :kv, :], (((1,), (1,)), ((), ())),
                                                preferred_element_type=F32)
        for c in range(2):
            m = None
            for j in range(qi + 1):
                cols = slice(j * TK, (j + 1) * TK)
                s = sc_ref[c, :, cols]
                if j == qi:
                    s = jnp.where(on_or_below_diag, s, NEG_INF)
                    sc_ref[c, :, cols] = s
                for g in range(groups):
                    sg = s[:, g * LANES:(g + 1) * LANES]
                    m = sg if m is None else jnp.maximum(m, sg)
            m_b = jnp.broadcast_to(jnp.max(m, axis=-1, keepdims=True), (TQ, LANES))
            m_b = jnp.concatenate([m_b] * groups, axis=-1)
            for j in range(qi + 1):
                cols = slice(j * TK, (j + 1) * TK)
                p_ref[c * TQ:(c + 1) * TQ, cols] = jnp.exp2(sc_ref[c, :, cols] - m_b).astype(BF16)
        acc = jnp.dot(p_ref[:, :kv], vb_ref[:kv, :], preferred_element_type=F32)
        hw = 2 * DA_HEAD_DIM
        o = (acc[:TQ, :hw] / acc[:TQ, hw:hw + 1]) - lam * (acc[TQ:, :hw] / acc[TQ:, hw:hw + 1])
        o_ref[0, qi * TQ:(qi + 1) * TQ, :] = (_rms(o, ng_ref[0]) * (1.0 - lambda_init)).astype(BF16)


def _attention(z, cos_t, sin_t, lam_p, ng, lambda_init, l):
    b, s, _ = z.shape
    hw = 2 * DA_HEAD_DIM
    q0 = (2 * RG_WIDTH) // hw
    k0 = q0 + DA_HEADS
    v0 = k0 + DA_HEADS
    nh = HEADS_PER_STEP
    assert q0 % nh == 0 and DA_HEADS % nh == 0
    seq = lambda c0: pl.BlockSpec((1, s, nh * hw), lambda bi, hi: (bi, 0, c0 // nh + hi))
    tab = pl.BlockSpec((1, s, hw), lambda bi, hi: (bi, 0, 0))
    return pl.pallas_call(
        functools.partial(_attn_kernel, lambda_init=lambda_init),
        grid=(b, DA_HEADS // nh),
        in_specs=[_layer(lam_p, l), seq(q0), seq(k0), seq(v0), tab, tab, _layer(ng, l)],
        out_specs=pl.BlockSpec((1, s, nh * hw), lambda bi, hi: (bi, 0, hi)),
        out_shape=jax.ShapeDtypeStruct((b, s, DA_WIDTH), BF16),
        scratch_shapes=[pltpu.VMEM((nh, s, hw), BF16), pltpu.VMEM((nh, 2, s, hw), BF16),
                        pltpu.VMEM((nh, s, 2 * hw), BF16), pltpu.VMEM((nh, 2, TQ, s), F32),
                        pltpu.VMEM((nh, 2 * TQ, s), BF16)],
        compiler_params=_params(2),
        name="diffattn",
    )(lam_p, z, z, z, cos_t, sin_t, ng)


def _mlstm_kernel(q_ref, k_ref, v_ref, o_ref, gcol_ref, grow_ref, cw_ref, cb_ref, brow_ref, bcol_ref,
                  ng_ref, y_ref, pq_ref, pk_ref, c_ref, m_ref):
    @pl.when(pl.program_id(1) == 0)
    def _():
        pq_ref[...] = jnp.zeros_like(pq_ref)
        pk_ref[...] = jnp.zeros_like(pk_ref)
        c_ref[...] = jnp.zeros_like(c_ref)
        m_ref[...] = jnp.zeros_like(m_ref)

    q_raw, k_raw = q_ref[0], k_ref[0]
    y_ref[0] = _mlstm_chunk(q_raw, k_raw, pq_ref[...], pk_ref[...], v_ref[0], o_ref[0], gcol_ref[0],
                            grow_ref[...], cw_ref, cb_ref, brow_ref, bcol_ref, ng_ref, c_ref, m_ref)
    pq_ref[...] = q_raw[L_ML - SUBLANES:]
    pk_ref[...] = k_raw[L_ML - SUBLANES:]


def _mlstm_chunk(q_raw, k_raw, pq, pk, vv, o_pre, gcol, grow, cw_ref, cb_ref, brow_ref, bcol_ref, ng_ref,
                 c_ref, m_ref):
    w = ML_WIDTH
    dh = ML_HEAD_DIM
    ln = q_raw.shape[0]
    qs = _silu(_causal_conv(q_raw, pq, cw_ref[0, :, :w], cb_ref[0, :, :w]))
    ks = _silu(_causal_conv(k_raw, pk, cw_ref[0, :, w:], cb_ref[0, :, w:])) * (dh ** -0.5)
    og = _sigmoid(o_pre)

    gc = gcol + brow_ref[...]
    gr = grow + bcol_ref[...]
    r_i = lax.broadcasted_iota(jnp.int32, (ln, ln), 0)
    c_i = lax.broadcasted_iota(jnp.int32, (ln, ln), 1)
    causal = r_i >= c_i
    b_cols = jnp.dot(causal.astype(F32), jax.nn.log_sigmoid(gc),
                     precision=lax.Precision.HIGHEST, preferred_element_type=F32)
    b_rows = jnp.dot(jax.nn.log_sigmoid(gr), (r_i <= c_i).astype(F32),
                     precision=lax.Precision.HIGHEST, preferred_element_type=F32)

    lane = lax.broadcasted_iota(jnp.int32, (1, LANES), 1)
    sub = lax.broadcasted_iota(jnp.int32, (SUBLANES, 1), 0)
    t_col = lax.broadcasted_iota(jnp.int32, (ln, 1), 0)

    def lane_pick(a, j):
        return jnp.sum(jnp.where(lane == j, a, 0.0), axis=-1, keepdims=True)

    def row_pick(a, j):
        return jnp.sum(jnp.where(sub == j, a, 0.0), axis=0, keepdims=True)

    outs = []
    for hh in range(ML_HEADS):
        sl = slice(hh * dh, (hh + 1) * dh)
        b_col = lane_pick(b_cols, ML_HEADS + hh)
        li_col = lane_pick(gc, hh)
        r_row = row_pick(gr, hh) - row_pick(b_rows, ML_HEADS + hh)
        m_prev = m_ref[hh]
        qh = qs[:, sl]
        kh = ks[:, sl]
        qb = qh.astype(BF16)
        kb = kh.astype(BF16)
        vb = jnp.concatenate([vv[:, sl].astype(BF16), jnp.ones((ln, dh), BF16)], axis=-1)
        c_prev = c_ref[hh]
        c_prev_b = c_prev.astype(BF16)

        def rows_out(r0, r1):
            b_part = b_col[r0:r1]
            m_inter = b_part + m_prev
            dmat = jnp.where(causal[r0:r1, :r1], b_part + r_row[:, :r1], -jnp.inf)
            m_t = jnp.maximum(m_inter, jnp.max(dmat, axis=-1, keepdims=True))
            w_intra = lax.dot_general(qb[r0:r1], kb[:r1], (((1,), (1,)), ((), ())),
                                      preferred_element_type=F32) * jnp.exp(dmat - m_t)
            num_den = (jnp.exp(m_inter - m_t) * jnp.dot(qb[r0:r1], c_prev_b, preferred_element_type=F32)
                       + jnp.dot(w_intra.astype(BF16), vb[:r1], preferred_element_type=F32))
            return num_den[:, :dh] / jnp.maximum(jnp.abs(num_den[:, dh:]), jnp.exp(-m_t))

        hout = jnp.concatenate([rows_out(0, ln // 2), rows_out(ln // 2, ln)], axis=0)
        outs.append(_rms(hout, ng_ref[0, :, sl]))
        b_last = jnp.sum(jnp.where(t_col == ln - 1, b_col, 0.0), axis=0, keepdims=True)
        g_col = b_last - b_col + li_col
        m_next = jnp.maximum(b_last + m_prev, jnp.max(g_col, axis=0, keepdims=True))
        decay = jnp.exp(b_last + m_prev - m_next)
        kw = kh * jnp.exp(g_col - m_next)
        c_ref[hh] = decay * c_prev + lax.dot_general(kw.astype(BF16), vb, (((0,), (0,)), ((), ())),
                                                     preferred_element_type=F32)
        m_ref[hh] = m_next
    return (jnp.concatenate(outs, axis=-1) * og).astype(BF16)


def _mlstm(z, zg, zgt, cw, cb, brow, bcol, ng, l):
    b, s, _ = z.shape
    w = ML_WIDTH
    nc = s // L_ML
    c0 = ML_Q0 // w
    col = lambda c: pl.BlockSpec((1, L_ML, w), lambda bi, ci: (bi, ci, c))
    return pl.pallas_call(
        _mlstm_kernel,
        grid=(b, nc),
        in_specs=[col(c0), col(c0 + 1), col(c0 + 2), col(c0 + 3),
                  pl.BlockSpec((1, L_ML, LANES), lambda bi, ci: (bi, ci, 0)),
                  pl.BlockSpec((SUBLANES, L_ML), lambda bi, ci: (0, bi * nc + ci)),
                  _layer(cw, l), _layer(cb, l), _full((1, LANES)), _full((SUBLANES, 1)), _layer(ng, l)],
        out_specs=col(0),
        out_shape=jax.ShapeDtypeStruct((b, s, w), BF16),
        scratch_shapes=[pltpu.VMEM((SUBLANES, w), F32), pltpu.VMEM((SUBLANES, w), F32),
                        pltpu.VMEM((ML_HEADS, ML_HEAD_DIM, 2 * ML_HEAD_DIM), F32),
                        pltpu.VMEM((ML_HEADS, 1, 1), F32)],
        compiler_params=_params(2),
        name="mlstm",
    )(z, z, z, z, zg, zgt, cw, cb, brow, bcol, ng)


def _outproj_kernel(x_ref, yr_ref, ya_ref, ym_ref, w_ref, g_ref, x1_ref, h_ref):
    half = TM_OUT // 2
    for r0 in (0, half):
        rows = slice(r0, r0 + half)
        acc = x_ref[rows, :]
        acc = acc + jnp.dot(yr_ref[rows, :], w_ref[0, 0:RG_WIDTH, :], preferred_element_type=F32)
        acc = acc + jnp.dot(ya_ref[rows, :], w_ref[0, RG_WIDTH:RG_WIDTH + DA_WIDTH, :], preferred_element_type=F32)
        acc = acc + jnp.dot(ym_ref[rows, :], w_ref[0, RG_WIDTH + DA_WIDTH:, :], preferred_element_type=F32)
        x1_ref[rows, :] = acc
        h_ref[rows, :] = _rms(acc, g_ref[0]).astype(BF16)


def _outproj(x2d, y_rg, y_da, y_ml, w_out, g, l):
    t, d = x2d.shape
    row = lambda wd: pl.BlockSpec((TM_OUT, wd), lambda i: (i, 0))
    return pl.pallas_call(
        _outproj_kernel,
        grid=(t // TM_OUT,),
        in_specs=[row(d), row(RG_WIDTH), row(DA_WIDTH), row(ML_WIDTH), _layer(w_out, 0), _layer(g, l)],
        out_specs=[row(d), row(d)],
        out_shape=[jax.ShapeDtypeStruct((t, d), F32), jax.ShapeDtypeStruct((t, d), BF16)],
        compiler_params=_params(1),
        name="outproj",
    )(x2d, y_rg, y_da, y_ml, w_out, g)


def _mlp_kernel(h_ref, x_ref, wup_ref, cw_ref, cb_ref, wdn_ref, ng_ref, o_ref, tail_ref, u_ref, act_ref,
                hs_ref, *, final_norm):
    @pl.when(pl.program_id(1) == 0)
    def _():
        tail_ref[...] = jnp.zeros_like(tail_ref)

    tm = h_ref.shape[1]
    hs_ref[...] = h_ref[0]
    n_ch = D_FF // CH_FF
    assert n_ch % 2 == 1 and n_ch * CH_FF == D_FF

    def cols(half, j):
        return pl.ds(half * D_FF + j * CH_FF, CH_FF)

    def up(j, slot):
        for half in range(2):
            u_ref[slot, half] = jnp.dot(hs_ref[...], wup_ref[0, :, cols(half, j)], preferred_element_type=F32)

    def gate(j, slot):
        out_cols = pl.ds(j * CH_FF, CH_FF)
        taps = [(cw_ref[0, :, cols(half, j)] * sc, cb_ref[0, :, cols(half, j)] * sc) for half, sc in ((0, 0.5), (1, 1.0))]
        for r0 in range(0, tm, RB_FF):
            halves = []
            for half in range(2):
                prev = tail_ref[:, cols(half, j)] if r0 == 0 else u_ref[slot, half, r0 - SUBLANES:r0, :]
                halves.append(_causal_conv(u_ref[slot, half, r0:r0 + RB_FF, :], prev, *taps[half]))
            g_half = halves[0]
            act_ref[r0:r0 + RB_FF, out_cols] = (g_half * (1.0 + jnp.tanh(g_half)) * halves[1]).astype(BF16)
        for half in range(2):
            tail_ref[:, cols(half, j)] = u_ref[slot, half, tm - SUBLANES:, :]

    up(0, 0)
    for j in range(n_ch):
        if j + 1 < n_ch:
            up(j + 1, (j + 1) % 2)
        gate(j, j % 2)

    y = x_ref[0] + jnp.dot(act_ref[...], wdn_ref[0], preferred_element_type=F32)
    if final_norm:
        y = _rms(y, ng_ref[...])
    o_ref[0] = y


def _mlp(h, x1, w_up_b, cw, cb, w_down_b, ng, final_norm, l):
    b, s, d = x1.shape
    tile = pl.BlockSpec((1, TM_MLP, d), lambda bi, ti: (bi, ti, 0))
    return pl.pallas_call(
        functools.partial(_mlp_kernel, final_norm=final_norm),
        grid=(b, s // TM_MLP),
        in_specs=[tile, tile, _layer(w_up_b, 0), _layer(cw, l), _layer(cb, l), _layer(w_down_b, 0), _full((1, d))],
        out_specs=tile,
        out_shape=jax.ShapeDtypeStruct((b, s, d), F32),
        scratch_shapes=[pltpu.VMEM((SUBLANES, 2 * D_FF), F32), pltpu.VMEM((2, 2, TM_MLP, CH_FF), F32),
                        pltpu.VMEM((TM_MLP, D_FF), BF16), pltpu.VMEM((TM_MLP, d), BF16)],
        compiler_params=_params(2),
        name="mlp",
    )(h, x1, w_up_b, cw, cb, w_down_b, ng)


def kernel(x, positions, attn_norm, w_in, rg_conv_w, rg_conv_b, rg_wa, rg_ba, rg_wx, rg_bx, rg_lambda, rg_norm, da_lambda, da_norm, ml_conv_w, ml_conv_b, ml_i_bias, ml_f_bias, ml_norm, w_out, mlp_norm, w_up, ffn_conv_w, ffn_conv_b, w_down, final_norm):
    b, s, d = x.shape
    depth = w_in.shape[0]
    t = b * s
    n_gate = 2 * ML_HEADS
    cos_t, sin_t = _rope_tables(positions)
    w_in_b, wa_b, wx_b = w_in.astype(BF16), rg_wa.astype(BF16), rg_wx.astype(BF16)
    w_gate_b = jnp.pad(w_in[:, :, D_MAIN:], ((0, 0), (0, 0), (0, LANES - n_gate))).astype(BF16)
    gate_bias = jnp.concatenate([ml_i_bias, ml_f_bias], axis=-1)
    attn_norm3, mlp_norm3, rg_norm3, da_norm3, ml_norm3 = map(_rows3, (attn_norm, mlp_norm, rg_norm, da_norm, ml_norm))
    rg_cb3, rg_ba3, rg_bx3, rg_lam3, ml_cb3, ffn_cb3 = map(_rows3, (rg_conv_b, rg_ba, rg_bx, rg_lambda, ml_conv_b, ffn_conv_b))
    for l in range(depth):
        lambda_init = 0.8 - 0.6 * math.exp(-0.3 * l)
        z, zg, zgt, w_up_b, w_down_b, w_out_b = _inproj(x.reshape(t, d), attn_norm3, w_in_b, w_gate_b[l],
                                                        w_up, w_down, w_out, l)
        z = z.reshape(b, s, D_MAIN)
        zg = zg.reshape(b, s, LANES)
        y_rg = _rglru(z, rg_conv_w, rg_cb3, wa_b, rg_ba3, wx_b, rg_bx3, rg_lam3, rg_norm3, l)
        y_da = _attention(z, cos_t, sin_t, da_lambda, da_norm3, lambda_init, l)
        y_ml = _mlstm(z, zg, zgt, ml_conv_w, ml_cb3,
                      jnp.pad(gate_bias[l], (0, LANES - n_gate)).reshape(1, LANES),
                      gate_bias[l].reshape(n_gate, 1), ml_norm3, l)
        x1, h2 = _outproj(x.reshape(t, d), y_rg.reshape(t, -1), y_da.reshape(t, -1), y_ml.reshape(t, -1),
                          w_out_b, mlp_norm3, l)
        x = _mlp(h2.reshape(b, s, d), x1.reshape(b, s, d), w_up_b, ffn_conv_w, ffn_cb3, w_down_b,
                 final_norm.reshape(1, d), l == depth - 1, l)
    return x
```
